```python
import jax, jax.numpy as jnp
from jax import lax
import numpy as np

D_MODEL = 1024
BATCH = 4
SEQ = 4096
DEPTH = 2

GRID_W = 64
CTX_LEN = 256

N_MIXERS = 2
N_A = (DEPTH + N_MIXERS - 1) // N_MIXERS
N_B = DEPTH // N_MIXERS
LAST_CTX_READER = (DEPTH - 1) - ((DEPTH - 1) % N_MIXERS)

ALPHA = (2 * DEPTH) ** 0.25
BETA = (8 * DEPTH) ** -0.25

GLA_HEADS = 4
GLA_DK = D_MODEL // 2
GLA_DV = D_MODEL
GLA_DK_HEAD = GLA_DK // GLA_HEADS
GLA_DV_HEAD = GLA_DV // GLA_HEADS
GLA_RANK = 16
GLA_GATE_NORM = 16.0
GLA_CHUNK = GRID_W
GLA_Q_SCALE = GLA_DK_HEAD ** -0.5
GLA_CTX_COLS = GLA_DK + GLA_DV + 2 * GLA_RANK
GLA_IN = GLA_CTX_COLS + GLA_DK + GLA_DV

GM_WIDTH = 3 * D_MODEL
GM_GROUPS = 4
GM_CHUNK = 128
GM_ROWS_PER_CHUNK = GM_CHUNK // GRID_W

FFN_HIDDEN = -(-8 * D_MODEL // (3 * 256)) * 256

kernel_name = 'hybrid_gla_gmlp_prefix_dit'


def layer_norm(x, g, b, eps=1e-5):
    xf = x.astype(jnp.float32)
    mu = xf.mean(-1, keepdims=True)
    var = jnp.square(xf - mu).mean(-1, keepdims=True)
    return ((xf - mu) * lax.rsqrt(var + eps) * g.astype(jnp.float32) + b.astype(jnp.float32)).astype(x.dtype)


def rms_norm_f32(x, g, eps=1e-6):
    return x * lax.rsqrt(jnp.mean(jnp.square(x), -1, keepdims=True) + eps) * g.astype(jnp.float32)


def adaln(cond, w, b):
    m = (jax.nn.silu(cond) @ w + b)[..., None, :]
    return jnp.split(m, 6, axis=-1)


def modulate(h, shift, scale):
    return h * (1.0 + scale) + shift


def flip(t):
    return jnp.flip(t, axis=1)


def gla_log_decay(a_lr, w_dec, b_dec):
    z = (a_lr @ w_dec + b_dec).astype(jnp.float32)
    g = jax.nn.log_sigmoid(z) / GLA_GATE_NORM
    return g.reshape(*g.shape[:-1], GLA_HEADS, GLA_DK_HEAD)


def gla_split_kva(p, w_dec, b_dec):
    B, L = p.shape[:2]
    o_v = GLA_DK
    o_a = GLA_DK + GLA_DV
    k = p[..., :o_v].astype(jnp.float32).reshape(B, L, GLA_HEADS, GLA_DK_HEAD)
    v = p[..., o_v:o_a].astype(jnp.float32).reshape(B, L, GLA_HEADS, GLA_DV_HEAD)
    g_f = gla_log_decay(p[..., o_a:o_a + GLA_RANK], w_dec[0], b_dec[0])
    g_b = gla_log_decay(p[..., o_a + GLA_RANK:GLA_CTX_COLS], w_dec[1], b_dec[1])
    return k, v, g_f, g_b


def gla_split_qr(p):
    B, L = p.shape[:2]
    q = p[..., GLA_CTX_COLS:GLA_CTX_COLS + GLA_DK].astype(jnp.float32)
    q = q.reshape(B, L, GLA_HEADS, GLA_DK_HEAD) * GLA_Q_SCALE
    r = p[..., GLA_CTX_COLS + GLA_DK:]
    return q, r


def to_chunks(t):
    B, L, H, d = t.shape
    return t.reshape(B, L // GLA_CHUNK, GLA_CHUNK, H, d).transpose(1, 0, 3, 2, 4)


def from_chunks(t):
    n, B, H, C, d = t.shape
    return t.transpose(1, 0, 3, 2, 4).reshape(B, n * C, H, d)


def gla_chunked(q, k, v, g, s0):
    qc, kc, vc, gc = to_chunks(q), to_chunks(k), to_chunks(v), to_chunks(g)
    b = jnp.cumsum(gc, axis=3)
    b_last = b[..., -1:, :]
    q_in = qc * jnp.exp(b)
    k_in = kc * jnp.exp(-b)
    k_st = kc * jnp.exp(b_last - b)
    mask = jnp.tril(jnp.ones((GLA_CHUNK, GLA_CHUNK), dtype=bool))
    att = jnp.where(mask, jnp.einsum('nbhik,nbhjk->nbhij', q_in, k_in), 0.0)
    o_intra = jnp.einsum('nbhij,nbhjv->nbhiv', att, vc)

    def step(S, inp):
        qi, ki, vi, dl = inp
        o = jnp.einsum('bhik,bhkv->bhiv', qi, S)
        S = S * jnp.exp(dl)[..., 0, :, None] + jnp.einsum('bhik,bhiv->bhkv', ki, vi)
        return S, o

    s_fin, o_inter = lax.scan(step, s0, (q_in, k_st, vc, b_last))
    return from_chunks(o_intra + o_inter), s_fin


def gla_final_state(k, v, g):
    b = jnp.cumsum(g, axis=1)
    return jnp.einsum('blhk,blhv->bhkv', k * jnp.exp(b[:, -1:] - b), v)


def gla_output(o, r, norm_g, w_out):
    B, L = o.shape[:2]
    o = rms_norm_f32(o, norm_g).reshape(B, L, GLA_DV)
    return (o * jax.nn.silu(r.astype(jnp.float32))).astype(r.dtype) @ w_out


def gla_mixer(a, ac, ctx_out, w_in, w_dec, b_dec, norm_g, w_out):
    B = a.shape[0]
    zeros = jnp.zeros((B, GLA_HEADS, GLA_DK_HEAD, GLA_DV_HEAD), jnp.float32)
    if ctx_out:
        pc = ac @ w_in
        kc, vc, gcf, gcb = gla_split_kva(pc, w_dec, b_dec)
        qc, rc = gla_split_qr(pc)
        o_cf, s_cf = gla_chunked(qc, kc, vc, gcf, zeros)
        o_cb, s_cb = gla_chunked(flip(qc), flip(kc), flip(vc), flip(gcb), zeros)
        yc = gla_output(o_cf + flip(o_cb), rc, norm_g, w_out)
    else:
        pc = ac @ w_in[:, :GLA_CTX_COLS]
        kc, vc, gcf, gcb = gla_split_kva(pc, w_dec, b_dec)
        s_cf = gla_final_state(kc, vc, gcf)
        s_cb = gla_final_state(flip(kc), flip(vc), flip(gcb))
        yc = None
    p = a @ w_in
    k, v, g_f, g_b = gla_split_kva(p, w_dec, b_dec)
    q, r = gla_split_qr(p)
    o_f, _ = gla_chunked(q, k, v, g_f, s_cf)
    o_b, _ = gla_chunked(flip(q), flip(k), flip(v), flip(g_b), s_cb)
    y = gla_output(o_f + flip(o_b), r, norm_g, w_out)
    return y, yc


def gmlp_chunk_mixer(h, n_chunks, w_in, ln_g, ln_b, w_s, b_s, w_out):
    B = h.shape[0]
    z = jax.nn.gelu(h @ w_in, approximate=False)
    u, v = jnp.split(z, 2, axis=-1)
    v = layer_norm(v, ln_g, ln_b)
    v = v.reshape(B, n_chunks, GM_CHUNK, GM_GROUPS, GM_WIDTH // GM_GROUPS)
    s = jnp.einsum('gpq,bnqgc->bnpgc', w_s, v) + b_s[:, :, None]
    return (u * s.reshape(u.shape)) @ w_out


def swiglu(h, w_gate, w_up, w_down):
    return (jax.nn.silu(h @ w_gate) * (h @ w_up)) @ w_down


def setup_inputs(seed: int = 0) -> dict:
    key = jax.random.key(seed)
    ks = jax.random.split(key, 24)
    D = D_MODEL

    def nrm(k, shape, s):
        return jax.random.normal(k, shape, jnp.float32) * s

    return {
        'x': nrm(ks[0], (BATCH, SEQ, D), 1.0),
        'c': nrm(ks[1], (BATCH, D), 1.0),
        'ctx': nrm(ks[2], (BATCH, CTX_LEN, D), 1.0),
        'c_ctx': nrm(ks[3], (D,), 1.0),
        'mod_w': nrm(ks[4], (DEPTH, D, 6 * D), 0.5 * D ** -0.5),
        'mod_b': nrm(ks[5], (DEPTH, 6 * D), 0.02),
        'ln_g': 1.0 + nrm(ks[6], (DEPTH, 2, D), 0.02),
        'ln_b': nrm(ks[7], (DEPTH, 2, D), 0.02),
        'gla_w_in': nrm(ks[8], (N_A, D, GLA_IN), D ** -0.5),
        'gla_w_decay': nrm(ks[9], (N_A, 2, GLA_RANK, GLA_DK), GLA_RANK ** -0.5),
        'gla_b_decay': 1.0 + nrm(ks[10], (N_A, 2, GLA_DK), 0.1),
        'gla_norm_g': 1.0 + nrm(ks[11], (N_A, GLA_DV_HEAD), 0.02),
        'gla_w_out': nrm(ks[12], (N_A, GLA_DV, D), BETA * GLA_DV ** -0.5),
        'gm_w_in': nrm(ks[13], (N_B, D, 2 * GM_WIDTH), D ** -0.5),
        'gm_ln_g': 1.0 + nrm(ks[14], (N_B, GM_WIDTH), 0.02),
        'gm_ln_b': nrm(ks[15], (N_B, GM_WIDTH), 0.02),
        'gm_w_s': nrm(ks[16], (N_B, GM_GROUPS, GM_CHUNK, GM_CHUNK), GM_CHUNK ** -0.5),
        'gm_b_s': 1.0 + nrm(ks[17], (N_B, GM_CHUNK, GM_GROUPS), 0.02),
        'gm_w_out': nrm(ks[18], (N_B, GM_WIDTH, D), BETA * GM_WIDTH ** -0.5),
        'ffn_w_gate': nrm(ks[19], (DEPTH, D, FFN_HIDDEN), D ** -0.5),
        'ffn_w_up': nrm(ks[20], (DEPTH, D, FFN_HIDDEN), D ** -0.5),
        'ffn_w_down': nrm(ks[21], (DEPTH, FFN_HIDDEN, D), BETA * FFN_HIDDEN ** -0.5),
    }


def reference(x, c, ctx, c_ctx, mod_w, mod_b, ln_g, ln_b, gla_w_in, gla_w_decay, gla_b_decay,
              gla_norm_g, gla_w_out, gm_w_in, gm_ln_g, gm_ln_b, gm_w_s, gm_b_s, gm_w_out,
              ffn_w_gate, ffn_w_up, ffn_w_down):
    rows = x.shape[1] // GRID_W
    n_lat_chunks = rows // GM_ROWS_PER_CHUNK
    h, hc = x, ctx
    for i in range(DEPTH):
        ctx_live = i <= LAST_CTX_READER
        ctx_out = i < LAST_CTX_READER
        j = i // N_MIXERS
        sh1, sc1, g1, sh2, sc2, g2 = adaln(c, mod_w[i], mod_b[i])
        if ctx_live:
            csh1, csc1, cg1, csh2, csc2, cg2 = adaln(c_ctx, mod_w[i], mod_b[i])
            ac = modulate(hc, csh1, csc1)
        a = modulate(h, sh1, sc1)
        if i % N_MIXERS == 0:
            y, yc = gla_mixer(a, ac, ctx_out, gla_w_in[j], gla_w_decay[j], gla_b_decay[j],
                              gla_norm_g[j], gla_w_out[j])
        else:
            y = gmlp_chunk_mixer(a, n_lat_chunks, gm_w_in[j], gm_ln_g[j], gm_ln_b[j],
                                 gm_w_s[j], gm_b_s[j], gm_w_out[j])
            if ctx_out:
                yc = gmlp_chunk_mixer(ac, hc.shape[1] // GM_CHUNK, gm_w_in[j], gm_ln_g[j], gm_ln_b[j],
                                      gm_w_s[j], gm_b_s[j], gm_w_out[j])
        h = layer_norm(ALPHA * h + g1 * y, ln_g[i, 0], ln_b[i, 0])
        if ctx_out:
            hc = layer_norm(ALPHA * hc + cg1 * yc, ln_g[i, 0], ln_b[i, 0])
        f = swiglu(modulate(h, sh2, sc2), ffn_w_gate[i], ffn_w_up[i], ffn_w_down[i])
        h = layer_norm(ALPHA * h + g2 * f, ln_g[i, 1], ln_b[i, 1])
        if ctx_out:
            fc = swiglu(modulate(hc, csh2, csc2), ffn_w_gate[i], ffn_w_up[i], ffn_w_down[i])
            hc = layer_norm(ALPHA * hc + cg2 * fc, ln_g[i, 1], ln_b[i, 1])
    return h
```

```python
import functools

import jax
import jax.numpy as jnp
from jax import lax
from jax.experimental import pallas as pl
from jax.experimental.pallas import tpu as pltpu

F32 = jnp.float32
BF16 = jnp.bfloat16

DEPTH = 2
ALPHA = (2 * DEPTH) ** 0.25
GLA_HEADS = 4
GLA_RANK = 16
GLA_GATE_NORM = 16.0
GLA_CHUNK = 64
GM_GROUPS = 4
GM_CHUNK = 128
COND_ROWS = 8
LANES = 128
VMEM_LIMIT = 56 * 1024 * 1024


def _dot(a, b):
    return jnp.dot(a, b, preferred_element_type=F32)


def _layer_norm(x, g, b, eps=1e-5):
    mu = jnp.mean(x, axis=-1, keepdims=True)
    xc = x - mu
    var = jnp.mean(xc * xc, axis=-1, keepdims=True)
    return xc * lax.rsqrt(var + eps) * g + b


def _silu(x):
    return x * jax.nn.sigmoid(x)


def _params(*sem):
    return pltpu.CompilerParams(dimension_semantics=sem, vmem_limit_bytes=VMEM_LIMIT)


def _const_spec(shape):
    nd = len(shape)
    return pl.BlockSpec(shape, lambda *_: (0,) * nd, pipeline_mode=pl.Buffered(1))


def _adaln_kernel(cond_ref, w_ref, b_ref, o_ref):
    s = _silu(cond_ref[...]).astype(BF16)
    o_ref[0] = _dot(s, w_ref[0].astype(BF16)) + b_ref[0]


def _adaln(cond, mod_w, mod_b):
    depth, d, n = mod_w.shape
    tn = n // 4
    return pl.pallas_call(
        _adaln_kernel,
        grid=(depth, n // tn),
        in_specs=[
            pl.BlockSpec((COND_ROWS, d), lambda i, j: (0, 0)),
            pl.BlockSpec((1, d, tn), lambda i, j: (i, 0, j)),
            pl.BlockSpec((1, 1, tn), lambda i, j: (i, 0, j)),
        ],
        out_specs=pl.BlockSpec((1, COND_ROWS, tn), lambda i, j: (i, 0, j)),
        out_shape=jax.ShapeDtypeStruct((depth, COND_ROWS, n), F32),
        compiler_params=_params("arbitrary", "arbitrary"),
        name="adaln",
    )(cond, mod_w, mod_b.reshape(depth, 1, n))


def _proj_kernel(x_ref, mod_ref, w_ref, *out_refs, segments):
    m = mod_ref[0]
    a = (x_ref[...] * (1.0 + m[1:2]) + m[0:1]).astype(BF16)
    for o_ref, (lo, hi, scale) in zip(out_refs, segments):
        y = _dot(a, w_ref[:, lo:hi])
        if scale != 1.0:
            y = y * scale
        o_ref[...] = y.astype(o_ref.dtype)


def _proj(x2d, mods, w, segments, out_dtypes, tm, mod_row):
    n, d = x2d.shape
    out_shape = [jax.ShapeDtypeStruct((n, hi - lo), dt) for (lo, hi, _), dt in zip(segments, out_dtypes)]
    out_specs = [pl.BlockSpec((tm, hi - lo), lambda i: (i, 0)) for (lo, hi, _) in segments]
    return pl.pallas_call(
        functools.partial(_proj_kernel, segments=segments),
        grid=(n // tm,),
        in_specs=[
            pl.BlockSpec((tm, d), lambda i: (i, 0)),
            pl.BlockSpec((1,) + mods.shape[1:], lambda i: (mod_row(i), 0, 0)),
            _const_spec(w.shape),
        ],
        out_specs=out_specs,
        out_shape=out_shape,
        compiler_params=_params("parallel"),
        name="gla_in_proj",
    )(x2d, mods, w)


def _log_sigmoid(z):
    return jnp.minimum(z, 0.0) - jnp.log1p(jnp.exp(-jnp.abs(z)))


def _scan_rows(kb, qb, vb, ab, wdec, bdec, s_ref, reverse, store_out):
    n, dk = kb.shape
    dv = vb.shape[1]
    dkh, dvh = dk // GLA_HEADS, dv // GLA_HEADS
    nc = n // GLA_CHUNK
    z = _dot(ab.astype(BF16), wdec) + bdec
    g = _log_sigmoid(z) * (1.0 / GLA_GATE_NORM)
    row = lax.broadcasted_iota(jnp.int32, (n, n), 0)
    col = lax.broadcasted_iota(jnp.int32, (n, n), 1)
    shift = GLA_CHUNK.bit_length() - 1
    same = jnp.right_shift(row, shift) == jnp.right_shift(col, shift)
    tri = jnp.where(same & ((col >= row) if reverse else (col <= row)), 1.0, 0.0).astype(F32)
    b = jnp.dot(tri, g, precision=lax.Precision.HIGHEST, preferred_element_type=F32)
    r64 = lax.broadcasted_iota(jnp.int32, (GLA_CHUNK, GLA_CHUNK), 0)
    c64 = lax.broadcasted_iota(jnp.int32, (GLA_CHUNK, GLA_CHUNK), 1)
    mask = (c64 >= r64) if reverse else (c64 <= r64)
    kf = kb.astype(F32)
    for c in (range(nc - 1, -1, -1) if reverse else range(nc)):
        lo = c * GLA_CHUNK
        b_c = b[lo:lo + GLA_CHUNK]
        b_end = b_c[0:1] if reverse else b_c[GLA_CHUNK - 1:GLA_CHUNK]
        k_c = kf[lo:lo + GLA_CHUNK]
        k_st_t = (k_c * jnp.exp(b_end - b_c)).T.astype(BF16)
        dec = jnp.exp(b_end)
        if store_out is not None:
            k_in = (k_c * jnp.exp(-b_c)).astype(BF16)
            q_in = (qb[lo:lo + GLA_CHUNK].astype(F32) * jnp.exp(b_c)).astype(BF16)
        outs = []
        for h in range(GLA_HEADS):
            ks = slice(h * dkh, (h + 1) * dkh)
            v_h = vb[lo:lo + GLA_CHUNK, h * dvh:(h + 1) * dvh]
            s_h = s_ref[h]
            if store_out is not None:
                att = lax.dot_general(q_in[:, ks], k_in[:, ks], (((1,), (1,)), ((), ())),
                                      preferred_element_type=F32)
                att = jnp.where(mask, att, 0.0).astype(BF16)
                outs.append(_dot(att, v_h) + _dot(q_in[:, ks], s_h.astype(BF16)))
            dcol = jnp.broadcast_to(dec[:, ks], (dkh, dkh)).T
            dcol = jnp.concatenate([dcol] * (dvh // dkh), axis=1)
            s_ref[h] = s_h * dcol + _dot(k_st_t[ks, :], v_h)
        if store_out is not None:
            store_out(c, jnp.concatenate(outs, axis=1))


def _gla_scan_kernel(*refs, reverse, epilogue):
    (k_ref, q_ref, v_ref, a_ref, kc_ref, vc_ref, ac_ref, wdec_ref, bdec_ref) = refs[:9]
    if epilogue:
        (ob_ref, r_ref, x_ref, mod_ref, ng_ref, wout_ref, lng_ref, lnb_ref, h_ref, s_ref, o_scr) = refs[9:]
    else:
        (o_ref, s_ref) = refs[9:]
    wdec = wdec_ref[...]
    bdec = bdec_ref[...]

    @pl.when(pl.program_id(1) == 0)
    def _():
        s_ref[...] = jnp.zeros_like(s_ref)
        _scan_rows(kc_ref[0], None, vc_ref[0], ac_ref[0], wdec, bdec, s_ref, reverse, None)

    if epilogue:
        def store(c, o):
            o_scr[c * GLA_CHUNK:(c + 1) * GLA_CHUNK, :] = o
    else:
        def store(c, o):
            o_ref[0, c * GLA_CHUNK:(c + 1) * GLA_CHUNK, :] = o
    _scan_rows(k_ref[0], q_ref[0], v_ref[0], a_ref[0], wdec, bdec, s_ref, reverse, store)

    if epilogue:
        o = o_scr[...] + ob_ref[0]
        dvh = o.shape[1] // GLA_HEADS
        ng = ng_ref[...]
        normed = []
        for h in range(GLA_HEADS):
            o_h = o[:, h * dvh:(h + 1) * dvh]
            ms = jnp.mean(o_h * o_h, axis=-1, keepdims=True)
            normed.append(o_h * lax.rsqrt(ms + 1e-6) * ng)
        on = jnp.concatenate(normed, axis=1)
        gated = (on * _silu(r_ref[0].astype(F32))).astype(BF16)
        y = _dot(gated, wout_ref[...])
        m = mod_ref[0]
        res = ALPHA * x_ref[0] + m[2:3] * y
        h_ref[0] = _layer_norm(res, lng_ref[...], lnb_ref[...])


def _gla_scan(k, q, v, a, kc, vc, ac, wdec, bdec, reverse, tb, epi=None):
    bsz, l, dk = k.shape
    dv = v.shape[2]
    lc = kc.shape[1]
    nblk = l // tb

    def blk(b, j):
        return (b, (nblk - 1 - j) if reverse else j, 0)

    def ctx(b, j):
        return (b, 0, 0)

    in_specs = [
        pl.BlockSpec((1, tb, dk), blk), pl.BlockSpec((1, tb, dk), blk),
        pl.BlockSpec((1, tb, dv), blk), pl.BlockSpec((1, tb, LANES), blk),
        pl.BlockSpec((1, lc, dk), ctx), pl.BlockSpec((1, lc, dv), ctx), pl.BlockSpec((1, lc, LANES), ctx),
        _const_spec(wdec.shape), _const_spec(bdec.shape),
    ]
    args = [k, q, v, a, kc, vc, ac, wdec, bdec]
    scratch = [pltpu.VMEM((GLA_HEADS, dk // GLA_HEADS, dv // GLA_HEADS), F32)]
    d_out = dv
    if epi is not None:
        o_b, r, x, mods, ng, wout, lng, lnb = epi
        d = x.shape[2]
        in_specs += [
            pl.BlockSpec((1, tb, dv), blk), pl.BlockSpec((1, tb, dv), blk), pl.BlockSpec((1, tb, d), blk),
            pl.BlockSpec((1,) + mods.shape[1:], ctx),
            _const_spec(ng.shape), _const_spec(wout.shape), _const_spec(lng.shape), _const_spec(lnb.shape),
        ]
        args += [o_b, r, x, mods, ng, wout, lng, lnb]
        scratch.append(pltpu.VMEM((tb, dv), F32))
        d_out = d
    return pl.pallas_call(
        functools.partial(_gla_scan_kernel, reverse=reverse, epilogue=epi is not None),
        grid=(bsz, nblk),
        in_specs=in_specs,
        out_specs=pl.BlockSpec((1, tb, d_out), blk),
        out_shape=jax.ShapeDtypeStruct((bsz, l, d_out), F32),
        scratch_shapes=scratch,
        compiler_params=_params("arbitrary", "arbitrary"),
        name="gla_scan_bwd" if reverse else "gla_scan_fwd",
    )(*args)


def _ffn_kernel(h_ref, mod_ref, wg_ref, wu_ref, wd_ref, lng_ref, lnb_ref, o_ref, hid_ref, *, chunk):
    m = mod_ref[0]
    h = h_ref[...]
    a = (h * (1.0 + m[4:5]) + m[3:4]).astype(BF16)
    hidden = wg_ref.shape[1]
    for lo in range(0, hidden, chunk):
        g = _dot(a, wg_ref[:, lo:lo + chunk])
        u = _dot(a, wu_ref[:, lo:lo + chunk])
        hid_ref[:, lo:lo + chunk] = (_silu(g) * u).astype(BF16)
    f = _dot(hid_ref[...], wd_ref[...])
    res = ALPHA * h + m[5:6] * f
    o_ref[...] = _layer_norm(res, lng_ref[...], lnb_ref[...])


def _ffn(h2d, mods, wg, wu, wd, lng, lnb, tm, tiles_per_batch):
    n, d = h2d.shape
    hidden = wg.shape[1]
    return pl.pallas_call(
        functools.partial(_ffn_kernel, chunk=256),
        grid=(n // tm,),
        in_specs=[
            pl.BlockSpec((tm, d), lambda i: (i, 0)),
            pl.BlockSpec((1,) + mods.shape[1:], lambda i: (i // tiles_per_batch, 0, 0)),
            _const_spec(wg.shape), _const_spec(wu.shape), _const_spec(wd.shape),
            _const_spec(lng.shape), _const_spec(lnb.shape),
        ],
        out_specs=pl.BlockSpec((tm, d), lambda i: (i, 0)),
        out_shape=jax.ShapeDtypeStruct((n, d), F32),
        scratch_shapes=[pltpu.VMEM((tm, hidden), BF16)],
        compiler_params=_params("parallel"),
        name="swiglu_ffn",
    )(h2d, mods, wg, wu, wd, lng, lnb)


def _gmlp_kernel(h_ref, mod_ref, win_ref, vg_ref, vb_ref, ws_ref, bs_ref, wout_ref, lng_ref, lnb_ref,
                 o_ref, u_scr, v_scr, gat_scr, *, chunk):
    m = mod_ref[0]
    h = h_ref[...]
    tm = h.shape[0]
    width = u_scr.shape[1]
    a = (h * (1.0 + m[1:2]) + m[0:1]).astype(BF16)
    for lo in range(0, 2 * width, chunk):
        z = _dot(a, win_ref[:, lo:lo + chunk])
        z = 0.5 * z * (1.0 + lax.erf(z * (0.5 ** 0.5)))
        if lo < width:
            u_scr[:, lo:lo + chunk] = z
        else:
            v_scr[:, lo - width:lo - width + chunk] = z
    v = v_scr[...]
    mu = jnp.mean(v, axis=-1, keepdims=True)
    vc = v - mu
    rstd = lax.rsqrt(jnp.mean(vc * vc, axis=-1, keepdims=True) + 1e-5)
    gw = width // GM_GROUPS
    for t in range(tm // GM_CHUNK):
        rows = slice(t * GM_CHUNK, (t + 1) * GM_CHUNK)
        for gi in range(GM_GROUPS):
            cols = slice(gi * gw, (gi + 1) * gw)
            vn = (v_scr[rows, cols] - mu[rows]) * rstd[rows] * vg_ref[:, cols] + vb_ref[:, cols]
            s = _dot(ws_ref[gi], vn.astype(BF16)) + jnp.concatenate([bs_ref[gi]] * (gw // LANES), axis=1)
            gat_scr[rows, cols] = (u_scr[rows, cols] * s).astype(BF16)
    y = _dot(gat_scr[...], wout_ref[...])
    res = ALPHA * h + m[2:3] * y
    o_ref[...] = _layer_norm(res, lng_ref[...], lnb_ref[...])


def _gmlp(h2d, mods, win, vg, vb, ws, bs, wout, lng, lnb, tm, tiles_per_batch):
    n, d = h2d.shape
    width = wout.shape[0]
    return pl.pallas_call(
        functools.partial(_gmlp_kernel, chunk=512),
        grid=(n // tm,),
        in_specs=[
            pl.BlockSpec((tm, d), lambda i: (i, 0)),
            pl.BlockSpec((1,) + mods.shape[1:], lambda i: (i // tiles_per_batch, 0, 0)),
            _const_spec(win.shape), _const_spec(vg.shape), _const_spec(vb.shape),
            _const_spec(ws.shape), _const_spec(bs.shape), _const_spec(wout.shape),
            _const_spec(lng.shape), _const_spec(lnb.shape),
        ],
        out_specs=pl.BlockSpec((tm, d), lambda i: (i, 0)),
        out_shape=jax.ShapeDtypeStruct((n, d), F32),
        scratch_shapes=[pltpu.VMEM((tm, width), F32), pltpu.VMEM((tm, width), F32),
                        pltpu.VMEM((tm, width), BF16)],
        compiler_params=_params("parallel"),
        name="gmlp_mixer",
    )(h2d, mods, win, vg, vb, ws, bs, wout, lng, lnb)


def kernel(x, c, ctx, c_ctx, mod_w, mod_b, ln_g, ln_b, gla_w_in, gla_w_decay, gla_b_decay, gla_norm_g,
           gla_w_out, gm_w_in, gm_ln_g, gm_ln_b, gm_w_s, gm_b_s, gm_w_out, ffn_w_gate, ffn_w_up, ffn_w_down):
    bsz, l, d = x.shape
    lc = ctx.shape[1]
    n = bsz * l
    assert bsz + 1 <= COND_ROWS
    dk = gla_w_decay.shape[-1]
    dv = gla_w_out.shape[1]
    q_scale = (dk // GLA_HEADS) ** -0.5

    cond = jnp.concatenate([c, c_ctx[None], jnp.zeros((COND_ROWS - bsz - 1, d), F32)], axis=0)
    mods = _adaln(cond, mod_w, mod_b).reshape(DEPTH, COND_ROWS, 6, d)

    w_in = gla_w_in[0]
    o_a = dk + dv
    o_q = o_a + 2 * GLA_RANK
    w_k, w_v, w_a = w_in[:, :dk], w_in[:, dk:o_a], w_in[:, o_a:o_q]
    w_q, w_r = w_in[:, o_q:o_q + dk], w_in[:, o_q + dk:]
    w_a = jnp.pad(w_a, ((0, 0), (0, LANES - 2 * GLA_RANK)))
    w_lat = jnp.concatenate([w_k, w_q, w_v, w_r, w_a], axis=1).astype(BF16)
    w_ctx = jnp.concatenate([w_k, w_v, w_a], axis=1).astype(BF16)
    seg_lat = ((0, dk, 1.0), (dk, 2 * dk, q_scale), (2 * dk, 2 * dk + dv, 1.0),
               (2 * dk + dv, 2 * dk + 2 * dv, 1.0), (2 * dk + 2 * dv, 2 * dk + 2 * dv + LANES, 1.0))
    seg_ctx = ((0, dk, 1.0), (dk, dk + dv, 1.0), (dk + dv, dk + dv + LANES, 1.0))
    tm = 512
    tpb = l // tm
    k, q, v, r, a = _proj(x.reshape(n, d), mods[0], w_lat, seg_lat, (BF16, BF16, BF16, BF16, F32), tm,
                          lambda i: i // tpb)
    kc, vc, ac = _proj(ctx.reshape(bsz * lc, d), mods[0], w_ctx, seg_ctx, (BF16, BF16, F32), lc,
                       lambda i: bsz)
    k, q, v, r, a = (t.reshape(bsz, l, -1) for t in (k, q, v, r, a))
    kc, vc, ac = (t.reshape(bsz, lc, -1) for t in (kc, vc, ac))

    wdec = jnp.zeros((2, LANES, dk), F32)
    wdec = wdec.at[0, :GLA_RANK].set(gla_w_decay[0, 0]).at[1, GLA_RANK:2 * GLA_RANK].set(gla_w_decay[0, 1])
    wdec = wdec.astype(BF16)
    bdec = gla_b_decay[0].reshape(2, 1, dk)
    tb = 256
    o_b = _gla_scan(k, q, v, a, kc, vc, ac, wdec[1], bdec[1], True, tb)
    ng = gla_norm_g[0].reshape(1, dv // GLA_HEADS)
    h = _gla_scan(k, q, v, a, kc, vc, ac, wdec[0], bdec[0], False, tb,
                  epi=(o_b, r, x, mods[0], ng, gla_w_out[0].astype(BF16),
                       ln_g[0, 0].reshape(1, d), ln_b[0, 0].reshape(1, d)))
    h = h.reshape(n, d)
    h = _ffn(h, mods[0], ffn_w_gate[0].astype(BF16), ffn_w_up[0].astype(BF16), ffn_w_down[0].astype(BF16),
             ln_g[0, 1].reshape(1, d), ln_b[0, 1].reshape(1, d), tm, tpb)

    tg = 256
    width = gm_w_out.shape[1]
    bs = jnp.broadcast_to(gm_b_s[0].T[:, :, None], (GM_GROUPS, GM_CHUNK, LANES))
    h = _gmlp(h, mods[1], gm_w_in[0].astype(BF16), gm_ln_g[0].reshape(1, width), gm_ln_b[0].reshape(1, width),
              gm_w_s[0].astype(BF16), bs, gm_w_out[0].astype(BF16),
              ln_g[1, 0].reshape(1, d), ln_b[1, 0].reshape(1, d), tg, l // tg)
    h = _ffn(h, mods[1], ffn_w_gate[1].astype(BF16), ffn_w_up[1].astype(BF16), ffn_w_down[1].astype(BF16),
             ln_g[1, 1].reshape(1, d), ln_b[1, 1].reshape(1, d), tm, tpb)
    return h.reshape(bsz, l, d)
```

```python
import functools

import jax
import jax.numpy as jnp
from jax import lax
from jax.experimental import pallas as pl
from jax.experimental.pallas import tpu as pltpu

F32 = jnp.float32
BF16 = jnp.bfloat16

DEPTH = 2
ALPHA = (2 * DEPTH) ** 0.25
GLA_HEADS = 4
GLA_RANK = 16
GLA_GATE_NORM = 16.0
GLA_CHUNK = 64
GM_GROUPS = 4
GM_CHUNK = 128
COND_ROWS = 8
LANES = 128
VMEM_LIMIT = 56 * 1024 * 1024

PROJ_TILE = 512
SCAN_BLOCK = 256
FFN_TILE = 512
FFN_COLS = 256
GMLP_TILE = 256
GMLP_COLS = 512


def _dot(a, b):
    return jnp.dot(a, b, preferred_element_type=F32)


def _layer_norm(x, g, b, eps=1e-5):
    mu = jnp.mean(x, axis=-1, keepdims=True)
    xc = x - mu
    var = jnp.mean(xc * xc, axis=-1, keepdims=True)
    return xc * lax.rsqrt(var + eps) * g + b


def _silu(x):
    return x * jax.nn.sigmoid(x)


def _params(*sem):
    return pltpu.CompilerParams(dimension_semantics=sem, vmem_limit_bytes=VMEM_LIMIT)


def _const_spec(shape):
    nd = len(shape)
    return pl.BlockSpec(shape, lambda *_: (0,) * nd, pipeline_mode=pl.Buffered(1))


def _adaln_kernel(cond_ref, w_ref, b_ref, o_ref):
    s = _silu(cond_ref[...]).astype(BF16)
    o_ref[0] = _dot(s, w_ref[0].astype(BF16)) + b_ref[0]


def _adaln(cond, mod_w, mod_b):
    depth, d, n = mod_w.shape
    tn = n // 4
    return pl.pallas_call(
        _adaln_kernel,
        grid=(depth, n // tn),
        in_specs=[
            pl.BlockSpec((COND_ROWS, d), lambda i, j: (0, 0)),
            pl.BlockSpec((1, d, tn), lambda i, j: (i, 0, j)),
            pl.BlockSpec((1, 1, tn), lambda i, j: (i, 0, j)),
        ],
        out_specs=pl.BlockSpec((1, COND_ROWS, tn), lambda i, j: (i, 0, j)),
        out_shape=jax.ShapeDtypeStruct((depth, COND_ROWS, n), F32),
        compiler_params=_params("arbitrary", "arbitrary"),
        name="adaln",
    )(cond, mod_w, mod_b.reshape(depth, 1, n))


def _log_sigmoid(z):
    return jnp.minimum(z, 0.0) - jnp.log(1.0 + jnp.exp(-jnp.abs(z)))


def _chunk_tri(n, reverse):
    row = lax.broadcasted_iota(jnp.int32, (n, n), 0)
    col = lax.broadcasted_iota(jnp.int32, (n, n), 1)
    shift = GLA_CHUNK.bit_length() - 1
    same = jnp.right_shift(row, shift) == jnp.right_shift(col, shift)
    return jnp.where(same & ((col >= row) if reverse else (col <= row)), 1.0, 0.0).astype(BF16)


def _gla_proj_kernel(x_ref, mod_ref, w_ref, wdec_ref, bdec_ref, *out_refs, latent, dk, dv, q_scale):
    m = mod_ref[0]
    a = (x_ref[...] * (1.0 + m[1:2]) + m[0:1]).astype(BF16)
    tm = a.shape[0]
    nblk = tm // SCAN_BLOCK
    ncb = SCAN_BLOCK // GLA_CHUNK
    o_a = 2 * dk if latent else dk
    o_v = o_a + LANES
    if latent:
        v_ref, gate_ref = out_refs[:2]
        dir_refs = (out_refs[2:6], out_refs[6:10])
        gate_ref[...] = _silu(_dot(a, w_ref[:, o_v + dv:o_v + 2 * dv])).astype(BF16)
    else:
        v_ref = out_refs[0]
        dir_refs = ((None, None) + tuple(out_refs[1:3]), (None, None) + tuple(out_refs[3:5]))
    v_ref[...] = _dot(a, w_ref[:, o_v:o_v + dv]).astype(BF16)
    tris = (_chunk_tri(SCAN_BLOCK, False), _chunk_tri(SCAN_BLOCK, True))
    for blk in range(nblk):
        rows = slice(blk * SCAN_BLOCK, (blk + 1) * SCAN_BLOCK)
        ab = a[rows]
        k = _dot(ab, w_ref[:, 0:dk])
        if latent:
            q = _dot(ab, w_ref[:, dk:2 * dk]) * q_scale
        a_lr = _dot(ab, w_ref[:, o_a:o_a + LANES]).astype(BF16)
        for rev in (0, 1):
            qin_ref, kin_ref, kst_ref, dec_ref = dir_refs[rev]
            z = _dot(a_lr, wdec_ref[rev]) + bdec_ref[rev]
            g = _log_sigmoid(z) * (1.0 / GLA_GATE_NORM)
            g_hi = g.astype(BF16)
            g_lo = (g - g_hi.astype(F32)).astype(BF16)
            b = _dot(tris[rev], g_hi) + _dot(tris[rev], g_lo)
            if latent:
                qin_ref[rows, :] = (q * jnp.exp(b)).astype(BF16)
                kin_ref[rows, :] = (k * jnp.exp(-b)).astype(BF16)
            ends = []
            for c in range(ncb):
                lo = c * GLA_CHUNK
                b_c = b[lo:lo + GLA_CHUNK]
                b_end = b_c[0:1] if rev else b_c[GLA_CHUNK - 1:GLA_CHUNK]
                kst = k[lo:lo + GLA_CHUNK] * jnp.exp(b_end - b_c)
                kst_ref[blk * ncb + c] = kst.T.astype(BF16)
                ends.append(b_end)
            dec_ref[blk] = jnp.exp(jnp.concatenate(ends, axis=0))


def _gla_proj(x2d, mods, w, wdec, bdec, mod_row, latent, tm, dk, dv, q_scale):
    n, d = x2d.shape
    ncb = SCAN_BLOCK // GLA_CHUNK
    row = lambda i: (i, 0)
    row3 = lambda i: (i, 0, 0)
    tok = lambda c: (jax.ShapeDtypeStruct((n, c), BF16), pl.BlockSpec((tm, c), row))
    kst = (jax.ShapeDtypeStruct((n // GLA_CHUNK, dk, GLA_CHUNK), BF16),
           pl.BlockSpec((tm // GLA_CHUNK, dk, GLA_CHUNK), row3))
    dec = (jax.ShapeDtypeStruct((n // SCAN_BLOCK, ncb, dk), F32),
           pl.BlockSpec((tm // SCAN_BLOCK, ncb, dk), row3))
    per_dir = [tok(dk), tok(dk), kst, dec] if latent else [kst, dec]
    outs = ([tok(dv), tok(dv)] if latent else [tok(dv)]) + per_dir + per_dir
    return pl.pallas_call(
        functools.partial(_gla_proj_kernel, latent=latent, dk=dk, dv=dv, q_scale=q_scale),
        grid=(n // tm,),
        in_specs=[
            pl.BlockSpec((tm, d), row),
            pl.BlockSpec((1,) + mods.shape[1:], lambda i: (mod_row(i), 0, 0)),
            _const_spec(w.shape), _const_spec(wdec.shape), _const_spec(bdec.shape),
        ],
        out_specs=[o[1] for o in outs],
        out_shape=[o[0] for o in outs],
        compiler_params=_params("parallel"),
        name="gla_in_proj" if latent else "gla_ctx_proj",
    )(x2d, mods, w, wdec, bdec)


def _state_step(s_ref, h, dec_row, kst_t, v_h):
    dkh, dvh = s_ref.shape[1:]
    dcol = jnp.broadcast_to(dec_row, (dkh, dkh)).T
    dcol = jnp.concatenate([dcol] * (dvh // dkh), axis=1)
    s_ref[h] = s_ref[h] * dcol + _dot(kst_t, v_h)


def _gla_scan_kernel(*refs, reverse, add_prev):
    qin_ref, kin_ref, kst_ref, v_ref, dec_ref, kstc_ref, vc_ref, decc_ref = refs[:8]
    if add_prev:
        prev_ref, o_ref, s_ref = refs[8:]
    else:
        o_ref, s_ref = refs[8:]
    dkh, dvh = s_ref.shape[1:]

    def order(nc):
        return range(nc - 1, -1, -1) if reverse else range(nc)

    @pl.when(pl.program_id(1) == 0)
    def _():
        s_ref[...] = jnp.zeros_like(s_ref)
        for c in order(vc_ref.shape[1] // GLA_CHUNK):
            for h in range(GLA_HEADS):
                _state_step(s_ref, h, decc_ref[0, c:c + 1, h * dkh:(h + 1) * dkh],
                            kstc_ref[c, h * dkh:(h + 1) * dkh, :],
                            vc_ref[0, c * GLA_CHUNK:(c + 1) * GLA_CHUNK, h * dvh:(h + 1) * dvh])

    r64 = lax.broadcasted_iota(jnp.int32, (GLA_CHUNK, GLA_CHUNK), 0)
    c64 = lax.broadcasted_iota(jnp.int32, (GLA_CHUNK, GLA_CHUNK), 1)
    mask = (c64 >= r64) if reverse else (c64 <= r64)
    for c in order(v_ref.shape[1] // GLA_CHUNK):
        rows = slice(c * GLA_CHUNK, (c + 1) * GLA_CHUNK)
        outs = []
        for h in range(GLA_HEADS):
            ks = slice(h * dkh, (h + 1) * dkh)
            q_in = qin_ref[0, rows, ks]
            v_h = v_ref[0, rows, h * dvh:(h + 1) * dvh]
            att = lax.dot_general(q_in, kin_ref[0, rows, ks], (((1,), (1,)), ((), ())),
                                  preferred_element_type=F32)
            att = jnp.where(mask, att, 0.0).astype(BF16)
            outs.append(_dot(att, v_h) + _dot(q_in, s_ref[h].astype(BF16)))
            _state_step(s_ref, h, dec_ref[0, c:c + 1, ks], kst_ref[c, ks, :], v_h)
        o = jnp.concatenate(outs, axis=1)
        if add_prev:
            o = o + prev_ref[0, rows, :]
        o_ref[0, rows, :] = o


def _gla_scan(qin, kin, kst, v, dec, kstc, vc, decc, reverse, prev=None):
    bsz, l, dk = qin.shape
    dv = v.shape[2]
    lc = vc.shape[1]
    tb = SCAN_BLOCK
    nblk = l // tb
    ncb = tb // GLA_CHUNK

    def pos(j):
        return (nblk - 1 - j) if reverse else j

    blk = lambda b, j: (b, pos(j), 0)
    flat = lambda b, j: (b * nblk + pos(j), 0, 0)
    ctx = lambda b, j: (b, 0, 0)
    in_specs = [
        pl.BlockSpec((1, tb, dk), blk), pl.BlockSpec((1, tb, dk), blk),
        pl.BlockSpec((ncb, dk, GLA_CHUNK), flat), pl.BlockSpec((1, tb, dv), blk),
        pl.BlockSpec((1, ncb, dk), flat),
        pl.BlockSpec((lc // GLA_CHUNK, dk, GLA_CHUNK), ctx), pl.BlockSpec((1, lc, dv), ctx),
        pl.BlockSpec((1, lc // GLA_CHUNK, dk), ctx),
    ]
    args = [qin, kin, kst, v, dec, kstc, vc, decc]
    if prev is not None:
        in_specs.append(pl.BlockSpec((1, tb, dv), blk))
        args.append(prev)
    return pl.pallas_call(
        functools.partial(_gla_scan_kernel, reverse=reverse, add_prev=prev is not None),
        grid=(bsz, nblk),
        in_specs=in_specs,
        out_specs=pl.BlockSpec((1, tb, dv), blk),
        out_shape=jax.ShapeDtypeStruct((bsz, l, dv), F32),
        scratch_shapes=[pltpu.VMEM((GLA_HEADS, dk // GLA_HEADS, dv // GLA_HEADS), F32)],
        compiler_params=_params("arbitrary", "arbitrary"),
        name="gla_scan_bwd" if reverse else "gla_scan_fwd",
    )(*args)


def _ffn_kernel(*refs, gla_prologue):
    if gla_prologue:
        (o_ref, gate_ref, x_ref, mod_ref, ng_ref, wout_ref, lng0_ref, lnb0_ref,
         wg_ref, wu_ref, wd_ref, lng_ref, lnb_ref, out_ref, hid_ref) = refs
        m = mod_ref[0]
        o = o_ref[...]
        dvh = o.shape[1] // GLA_HEADS
        normed = []
        for hd in range(GLA_HEADS):
            o_h = o[:, hd * dvh:(hd + 1) * dvh]
            ms = jnp.mean(o_h * o_h, axis=-1, keepdims=True)
            normed.append(o_h * lax.rsqrt(ms + 1e-6) * ng_ref[...])
        gated = (jnp.concatenate(normed, axis=1) * gate_ref[...].astype(F32)).astype(BF16)
        y = _dot(gated, wout_ref[...])
        h = _layer_norm(ALPHA * x_ref[...] + m[2:3] * y, lng0_ref[...], lnb0_ref[...])
    else:
        h_ref, mod_ref, wg_ref, wu_ref, wd_ref, lng_ref, lnb_ref, out_ref, hid_ref = refs
        m = mod_ref[0]
        h = h_ref[...]
    a = (h * (1.0 + m[4:5]) + m[3:4]).astype(BF16)
    hidden = wg_ref.shape[1]
    for lo in range(0, hidden, FFN_COLS):
        g = _dot(a, wg_ref[:, lo:lo + FFN_COLS])
        u = _dot(a, wu_ref[:, lo:lo + FFN_COLS])
        hid_ref[:, lo:lo + FFN_COLS] = (_silu(g) * u).astype(BF16)
    f = _dot(hid_ref[...], wd_ref[...])
    res = ALPHA * h + m[5:6] * f
    out_ref[...] = _layer_norm(res, lng_ref[...], lnb_ref[...])


def _ffn(acts, mods, consts, tiles_per_batch, gla_prologue):
    n, d = acts[0].shape
    tm = FFN_TILE
    hidden = consts[-5].shape[1]
    row = lambda i: (i, 0)
    mod_spec = pl.BlockSpec((1,) + mods.shape[1:], lambda i: (i // tiles_per_batch, 0, 0))
    in_specs = ([pl.BlockSpec((tm, t.shape[1]), row) for t in acts] + [mod_spec]
                + [_const_spec(t.shape) for t in consts])
    return pl.pallas_call(
        functools.partial(_ffn_kernel, gla_prologue=gla_prologue),
        grid=(n // tm,),
        in_specs=in_specs,
        out_specs=pl.BlockSpec((tm, d), row),
        out_shape=jax.ShapeDtypeStruct((n, d), F32),
        scratch_shapes=[pltpu.VMEM((tm, hidden), BF16)],
        compiler_params=_params("parallel"),
        name="gla_out_ffn" if gla_prologue else "swiglu_ffn",
    )(*acts, mods, *consts)


def _gmlp_kernel(h_ref, mod_ref, win_ref, vg_ref, vb_ref, ws_ref, bs_ref, wout_ref, lng_ref, lnb_ref,
                 o_ref, u_scr, v_scr, gat_scr):
    m = mod_ref[0]
    h = h_ref[...]
    tm = h.shape[0]
    width = u_scr.shape[1]
    a = (h * (1.0 + m[1:2]) + m[0:1]).astype(BF16)
    for lo in range(0, 2 * width, GMLP_COLS):
        z = _dot(a, win_ref[:, lo:lo + GMLP_COLS])
        z = 0.5 * z * (1.0 + lax.erf(z * (0.5 ** 0.5)))
        if lo < width:
            u_scr[:, lo:lo + GMLP_COLS] = z
        else:
            v_scr[:, lo - width:lo - width + GMLP_COLS] = z
    v = v_scr[...]
    mu = jnp.mean(v, axis=-1, keepdims=True)
    vc = v - mu
    rstd = lax.rsqrt(jnp.mean(vc * vc, axis=-1, keepdims=True) + 1e-5)
    gw = width // GM_GROUPS
    for t in range(tm // GM_CHUNK):
        rows = slice(t * GM_CHUNK, (t + 1) * GM_CHUNK)
        for gi in range(GM_GROUPS):
            cols = slice(gi * gw, (gi + 1) * gw)
            vn = (v_scr[rows, cols] - mu[rows]) * rstd[rows] * vg_ref[:, cols] + vb_ref[:, cols]
            s = _dot(ws_ref[gi], vn.astype(BF16)) + jnp.concatenate([bs_ref[gi]] * (gw // LANES), axis=1)
            gat_scr[rows, cols] = (u_scr[rows, cols] * s).astype(BF16)
    y = _dot(gat_scr[...], wout_ref[...])
    res = ALPHA * h + m[2:3] * y
    o_ref[...] = _layer_norm(res, lng_ref[...], lnb_ref[...])


def _gmlp(h2d, mods, consts, tiles_per_batch):
    n, d = h2d.shape
    tm = GMLP_TILE
    width = consts[5].shape[0]
    row = lambda i: (i, 0)
    return pl.pallas_call(
        _gmlp_kernel,
        grid=(n // tm,),
        in_specs=[
            pl.BlockSpec((tm, d), row),
            pl.BlockSpec((1,) + mods.shape[1:], lambda i: (i // tiles_per_batch, 0, 0)),
        ] + [_const_spec(t.shape) for t in consts],
        out_specs=pl.BlockSpec((tm, d), row),
        out_shape=jax.ShapeDtypeStruct((n, d), F32),
        scratch_shapes=[pltpu.VMEM((tm, width), F32), pltpu.VMEM((tm, width), F32),
                        pltpu.VMEM((tm, width), BF16)],
        compiler_params=_params("parallel"),
        name="gmlp_mixer",
    )(h2d, mods, *consts)


def kernel(x, c, ctx, c_ctx, mod_w, mod_b, ln_g, ln_b, gla_w_in, gla_w_decay, gla_b_decay, gla_norm_g,
           gla_w_out, gm_w_in, gm_ln_g, gm_ln_b, gm_w_s, gm_b_s, gm_w_out, ffn_w_gate, ffn_w_up, ffn_w_down):
    bsz, l, d = x.shape
    lc = ctx.shape[1]
    n = bsz * l
    assert bsz + 1 <= COND_ROWS
    dk = gla_w_decay.shape[-1]
    dv = gla_w_out.shape[1]
    q_scale = (dk // GLA_HEADS) ** -0.5
    vec = lambda t: t.reshape(1, -1)

    cond = jnp.concatenate([c, c_ctx[None], jnp.zeros((COND_ROWS - bsz - 1, d), F32)], axis=0)
    mods = _adaln(cond, mod_w, mod_b).reshape(DEPTH, COND_ROWS, 6, d)

    w_in = gla_w_in[0]
    o_a = dk + dv
    o_q = o_a + 2 * GLA_RANK
    w_k, w_v, w_a = w_in[:, :dk], w_in[:, dk:o_a], w_in[:, o_a:o_q]
    w_q, w_r = w_in[:, o_q:o_q + dk], w_in[:, o_q + dk:]
    w_a = jnp.pad(w_a, ((0, 0), (0, LANES - 2 * GLA_RANK)))
    w_lat = jnp.concatenate([w_k, w_q, w_a, w_v, w_r], axis=1).astype(BF16)
    w_ctx = jnp.concatenate([w_k, w_a, w_v], axis=1).astype(BF16)
    wdec = jnp.zeros((2, LANES, dk), F32)
    wdec = wdec.at[0, :GLA_RANK].set(gla_w_decay[0, 0]).at[1, GLA_RANK:2 * GLA_RANK].set(gla_w_decay[0, 1])
    wdec = wdec.astype(BF16)
    bdec = gla_b_decay[0].reshape(2, 1, dk)

    tpb = l // PROJ_TILE
    (v, gate, qin_f, kin_f, kst_f, dec_f, qin_b, kin_b, kst_b, dec_b) = _gla_proj(
        x.reshape(n, d), mods[0], w_lat, wdec, bdec, lambda i: i // tpb, True, PROJ_TILE, dk, dv, q_scale)
    (vc, kstc_f, decc_f, kstc_b, decc_b) = _gla_proj(
        ctx.reshape(bsz * lc, d), mods[0], w_ctx, wdec, bdec, lambda i: bsz, False, lc, dk, dv, q_scale)
    b3 = lambda t: t.reshape(bsz, l, -1)
    v3, vc3 = b3(v), vc.reshape(bsz, lc, dv)
    o_b = _gla_scan(b3(qin_b), b3(kin_b), kst_b, v3, dec_b, kstc_b, vc3, decc_b, True)
    o = _gla_scan(b3(qin_f), b3(kin_f), kst_f, v3, dec_f, kstc_f, vc3, decc_f, False, prev=o_b)

    ffn_consts = lambda i: (ffn_w_gate[i].astype(BF16), ffn_w_up[i].astype(BF16), ffn_w_down[i].astype(BF16),
                            vec(ln_g[i, 1]), vec(ln_b[i, 1]))
    gla_consts = (vec(gla_norm_g[0]), gla_w_out[0].astype(BF16), vec(ln_g[0, 0]), vec(ln_b[0, 0]))
    tpf = l // FFN_TILE
    h = _ffn((o.reshape(n, dv), gate, x.reshape(n, d)), mods[0], gla_consts + ffn_consts(0), tpf, True)

    bs = jnp.broadcast_to(gm_b_s[0].T[:, :, None], (GM_GROUPS, GM_CHUNK, LANES))
    gm_consts = (gm_w_in[0].astype(BF16), vec(gm_ln_g[0]), vec(gm_ln_b[0]), gm_w_s[0].astype(BF16), bs,
                 gm_w_out[0].astype(BF16), vec(ln_g[1, 0]), vec(ln_b[1, 0]))
    h = _gmlp(h, mods[1], gm_consts, l // GMLP_TILE)
    h = _ffn((h,), mods[1], ffn_consts(1), tpf, False)
    return h.reshape(bsz, l, d)
```

```python
import functools

import jax
import jax.numpy as jnp
from jax import lax
from jax.experimental import pallas as pl
from jax.experimental.pallas import tpu as pltpu

F32 = jnp.float32
BF16 = jnp.bfloat16

DEPTH = 2
ALPHA = (2 * DEPTH) ** 0.25
GLA_HEADS = 4
GLA_RANK = 16
GLA_GATE_NORM = 16.0
GLA_CHUNK = 64
GM_GROUPS = 4
GM_CHUNK = 128
COND_ROWS = 8
LANES = 128
VMEM_LIMIT = 56 * 1024 * 1024

PROJ_TILE = 512
SCAN_BLOCK = 256
FFN_TILE = 512
FFN_COLS = 256
GMLP_TILE = 256
GMLP_COLS = 512


def _dot(a, b):
    return jnp.dot(a, b, preferred_element_type=F32)


def _layer_norm(x, g, b, eps=1e-5):
    mu = jnp.mean(x, axis=-1, keepdims=True)
    xc = x - mu
    var = jnp.mean(xc * xc, axis=-1, keepdims=True)
    return xc * lax.rsqrt(var + eps) * g + b


def _silu(x):
    return x * jax.nn.sigmoid(x)


def _params(*sem):
    return pltpu.CompilerParams(dimension_semantics=sem, vmem_limit_bytes=VMEM_LIMIT)


def _const_spec(shape):
    nd = len(shape)
    return pl.BlockSpec(shape, lambda *_: (0,) * nd, pipeline_mode=pl.Buffered(1))


def _adaln_kernel(cond_ref, w_ref, b_ref, o_ref):
    s = _silu(cond_ref[...]).astype(BF16)
    o_ref[0] = _dot(s, w_ref[0].astype(BF16)) + b_ref[0]


def _adaln(cond, mod_w, mod_b):
    depth, d, n = mod_w.shape
    tn = n // 4
    return pl.pallas_call(
        _adaln_kernel,
        grid=(depth, n // tn),
        in_specs=[
            pl.BlockSpec((COND_ROWS, d), lambda i, j: (0, 0)),
            pl.BlockSpec((1, d, tn), lambda i, j: (i, 0, j)),
            pl.BlockSpec((1, 1, tn), lambda i, j: (i, 0, j)),
        ],
        out_specs=pl.BlockSpec((1, COND_ROWS, tn), lambda i, j: (i, 0, j)),
        out_shape=jax.ShapeDtypeStruct((depth, COND_ROWS, n), F32),
        compiler_params=_params("arbitrary", "arbitrary"),
        name="adaln",
    )(cond, mod_w, mod_b.reshape(depth, 1, n))


def _log_sigmoid(z):
    return jnp.minimum(z, 0.0) - jnp.log(1.0 + jnp.exp(-jnp.abs(z)))


def _chunk_tri(n, reverse):
    row = lax.broadcasted_iota(jnp.int32, (n, n), 0)
    col = lax.broadcasted_iota(jnp.int32, (n, n), 1)
    shift = GLA_CHUNK.bit_length() - 1
    same = jnp.right_shift(row, shift) == jnp.right_shift(col, shift)
    return jnp.where(same & ((col >= row) if reverse else (col <= row)), 1.0, 0.0).astype(BF16)


def _gla_proj_kernel(x_ref, mod_ref, w_ref, wdec_ref, bdec_ref, *out_refs, latent, dk, dv, q_scale):
    m = mod_ref[0]
    a = (x_ref[...] * (1.0 + m[1:2]) + m[0:1]).astype(BF16)
    tm = a.shape[0]
    nblk = tm // SCAN_BLOCK
    ncb = SCAN_BLOCK // GLA_CHUNK
    o_a = 2 * dk if latent else dk
    o_v = o_a + LANES
    if latent:
        v_ref, gate_ref = out_refs[:2]
        dir_refs = (out_refs[2:6], out_refs[6:10])
    else:
        v_ref = out_refs[0]
        dir_refs = ((None, None) + tuple(out_refs[1:3]), (None, None) + tuple(out_refs[3:5]))
    k = _dot(a, w_ref[:, 0:dk])
    if latent:
        q = _dot(a, w_ref[:, dk:2 * dk]) * q_scale
    a_lr = _dot(a, w_ref[:, o_a:o_a + LANES]).astype(BF16)
    z = [_dot(a_lr, wdec_ref[rev]) + bdec_ref[rev] for rev in (0, 1)]
    v_ref[...] = _dot(a, w_ref[:, o_v:o_v + dv]).astype(BF16)
    g_parts = []
    for rev in (0, 1):
        g = _log_sigmoid(z[rev]) * (1.0 / GLA_GATE_NORM)
        g_hi = g.astype(BF16)
        g_parts.append((g_hi, (g - g_hi.astype(F32)).astype(BF16)))
    tris = (_chunk_tri(SCAN_BLOCK, False), _chunk_tri(SCAN_BLOCK, True))
    b = {}
    for rev in (0, 1):
        for blk in range(nblk):
            rows = slice(blk * SCAN_BLOCK, (blk + 1) * SCAN_BLOCK)
            b[rev, blk] = _dot(tris[rev], g_parts[rev][0][rows]) + _dot(tris[rev], g_parts[rev][1][rows])
    if latent:
        gate_ref[...] = _silu(_dot(a, w_ref[:, o_v + dv:o_v + 2 * dv])).astype(BF16)
    for rev in (0, 1):
        qin_ref, kin_ref, kst_ref, dec_ref = dir_refs[rev]
        for blk in range(nblk):
            rows = slice(blk * SCAN_BLOCK, (blk + 1) * SCAN_BLOCK)
            bb = b[rev, blk]
            if latent:
                qin_ref[rows, :] = (q[rows] * jnp.exp(bb)).astype(BF16)
                kin_ref[rows, :] = (k[rows] * jnp.exp(-bb)).astype(BF16)
            ends = []
            for c in range(ncb):
                lo = c * GLA_CHUNK
                b_c = bb[lo:lo + GLA_CHUNK]
                b_end = b_c[0:1] if rev else b_c[GLA_CHUNK - 1:GLA_CHUNK]
                kst = k[blk * SCAN_BLOCK + lo:blk * SCAN_BLOCK + lo + GLA_CHUNK] * jnp.exp(b_end - b_c)
                kst_ref[blk * ncb + c] = kst.T.astype(BF16)
                ends.append(b_end)
            dec_ref[blk] = jnp.exp(jnp.concatenate(ends, axis=0))


def _gla_proj(x2d, mods, w, wdec, bdec, mod_row, latent, tm, dk, dv, q_scale):
    n, d = x2d.shape
    ncb = SCAN_BLOCK // GLA_CHUNK
    row = lambda i: (i, 0)
    row3 = lambda i: (i, 0, 0)
    tok = lambda c: (jax.ShapeDtypeStruct((n, c), BF16), pl.BlockSpec((tm, c), row))
    kst = (jax.ShapeDtypeStruct((n // GLA_CHUNK, dk, GLA_CHUNK), BF16),
           pl.BlockSpec((tm // GLA_CHUNK, dk, GLA_CHUNK), row3))
    dec = (jax.ShapeDtypeStruct((n // SCAN_BLOCK, ncb, dk), F32),
           pl.BlockSpec((tm // SCAN_BLOCK, ncb, dk), row3))
    per_dir = [tok(dk), tok(dk), kst, dec] if latent else [kst, dec]
    outs = ([tok(dv), tok(dv)] if latent else [tok(dv)]) + per_dir + per_dir
    return pl.pallas_call(
        functools.partial(_gla_proj_kernel, latent=latent, dk=dk, dv=dv, q_scale=q_scale),
        grid=(n // tm,),
        in_specs=[
            pl.BlockSpec((tm, d), row),
            pl.BlockSpec((1,) + mods.shape[1:], lambda i: (mod_row(i), 0, 0)),
            _const_spec(w.shape), _const_spec(wdec.shape), _const_spec(bdec.shape),
        ],
        out_specs=[o[1] for o in outs],
        out_shape=[o[0] for o in outs],
        compiler_params=_params("parallel"),
        name="gla_in_proj" if latent else "gla_ctx_proj",
    )(x2d, mods, w, wdec, bdec)


def _decay_columns(dec_row, dvh):
    dkh = dec_row.shape[1]
    dcol = jnp.broadcast_to(dec_row, (dkh, dkh)).T
    return jnp.concatenate([dcol] * (dvh // dkh), axis=1)


def _gla_scan_kernel(*refs, reverse, add_prev):
    qin_ref, kin_ref, kst_ref, v_ref, dec_ref, kstc_ref, vc_ref, decc_ref = refs[:8]
    if add_prev:
        prev_ref, o_ref, s_ref = refs[8:]
    else:
        o_ref, s_ref = refs[8:]
    dkh, dvh = s_ref.shape[1:]
    heads = range(GLA_HEADS)
    ks = [slice(h * dkh, (h + 1) * dkh) for h in heads]
    vs = [slice(h * dvh, (h + 1) * dvh) for h in heads]

    def order(nc):
        return list(range(nc - 1, -1, -1) if reverse else range(nc))

    def rows(c):
        return slice(c * GLA_CHUNK, (c + 1) * GLA_CHUNK)

    @pl.when(pl.program_id(1) == 0)
    def _():
        chunks = order(vc_ref.shape[1] // GLA_CHUNK)
        upd = {(c, h): _dot(kstc_ref[c, ks[h], :], vc_ref[0, rows(c), vs[h]]) for c in chunks for h in heads}
        for h in heads:
            s = jnp.zeros((dkh, dvh), F32)
            for c in chunks:
                s = s * _decay_columns(decc_ref[0, c:c + 1, ks[h]], dvh) + upd[c, h]
            s_ref[h] = s

    chunks = order(v_ref.shape[1] // GLA_CHUNK)
    pairs = [(c, h) for c in chunks for h in heads]
    r64 = lax.broadcasted_iota(jnp.int32, (GLA_CHUNK, GLA_CHUNK), 0)
    c64 = lax.broadcasted_iota(jnp.int32, (GLA_CHUNK, GLA_CHUNK), 1)
    mask = (c64 >= r64) if reverse else (c64 <= r64)
    att = {(c, h): lax.dot_general(qin_ref[0, rows(c), ks[h]], kin_ref[0, rows(c), ks[h]],
                                   (((1,), (1,)), ((), ())), preferred_element_type=F32) for c, h in pairs}
    upd = {(c, h): _dot(kst_ref[c, ks[h], :], v_ref[0, rows(c), vs[h]]) for c, h in pairs}
    dcol = {(c, h): _decay_columns(dec_ref[0, c:c + 1, ks[h]], dvh) for c, h in pairs}
    att = {p: jnp.where(mask, att[p], 0.0).astype(BF16) for p in pairs}
    intra = {(c, h): _dot(att[c, h], v_ref[0, rows(c), vs[h]]) for c, h in pairs}
    state = [s_ref[h] for h in heads]
    for c in chunks:
        outs = []
        for h in heads:
            outs.append(intra[c, h] + _dot(qin_ref[0, rows(c), ks[h]], state[h].astype(BF16)))
            state[h] = state[h] * dcol[c, h] + upd[c, h]
        o = jnp.concatenate(outs, axis=1)
        if add_prev:
            o = o + prev_ref[0, rows(c), :]
        o_ref[0, rows(c), :] = o
    for h in heads:
        s_ref[h] = state[h]


def _gla_scan(qin, kin, kst, v, dec, kstc, vc, decc, reverse, prev=None):
    bsz, l, dk = qin.shape
    dv = v.shape[2]
    lc = vc.shape[1]
    tb = SCAN_BLOCK
    nblk = l // tb
    ncb = tb // GLA_CHUNK

    def pos(j):
        return (nblk - 1 - j) if reverse else j

    blk = lambda b, j: (b, pos(j), 0)
    flat = lambda b, j: (b * nblk + pos(j), 0, 0)
    ctx = lambda b, j: (b, 0, 0)
    in_specs = [
        pl.BlockSpec((1, tb, dk), blk), pl.BlockSpec((1, tb, dk), blk),
        pl.BlockSpec((ncb, dk, GLA_CHUNK), flat), pl.BlockSpec((1, tb, dv), blk),
        pl.BlockSpec((1, ncb, dk), flat),
        pl.BlockSpec((lc // GLA_CHUNK, dk, GLA_CHUNK), ctx), pl.BlockSpec((1, lc, dv), ctx),
        pl.BlockSpec((1, lc // GLA_CHUNK, dk), ctx),
    ]
    args = [qin, kin, kst, v, dec, kstc, vc, decc]
    if prev is not None:
        in_specs.append(pl.BlockSpec((1, tb, dv), blk))
        args.append(prev)
    return pl.pallas_call(
        functools.partial(_gla_scan_kernel, reverse=reverse, add_prev=prev is not None),
        grid=(bsz, nblk),
        in_specs=in_specs,
        out_specs=pl.BlockSpec((1, tb, dv), blk),
        out_shape=jax.ShapeDtypeStruct((bsz, l, dv), F32),
        scratch_shapes=[pltpu.VMEM((GLA_HEADS, dk // GLA_HEADS, dv // GLA_HEADS), F32)],
        compiler_params=_params("arbitrary", "arbitrary"),
        name="gla_scan_bwd" if reverse else "gla_scan_fwd",
    )(*args)


def _ffn_kernel(*refs, gla_prologue):
    if gla_prologue:
        (o_ref, gate_ref, x_ref, mod_ref, ng_ref, wout_ref, lng0_ref, lnb0_ref,
         wg_ref, wu_ref, wd_ref, lng_ref, lnb_ref, out_ref, hid_ref) = refs
        m = mod_ref[0]
        o = o_ref[...]
        dvh = o.shape[1] // GLA_HEADS
        normed = []
        for hd in range(GLA_HEADS):
            o_h = o[:, hd * dvh:(hd + 1) * dvh]
            ms = jnp.mean(o_h * o_h, axis=-1, keepdims=True)
            normed.append(o_h * lax.rsqrt(ms + 1e-6) * ng_ref[...])
        gated = (jnp.concatenate(normed, axis=1) * gate_ref[...].astype(F32)).astype(BF16)
        y = _dot(gated, wout_ref[...])
        h = _layer_norm(ALPHA * x_ref[...] + m[2:3] * y, lng0_ref[...], lnb0_ref[...])
    else:
        h_ref, mod_ref, wg_ref, wu_ref, wd_ref, lng_ref, lnb_ref, out_ref, hid_ref = refs
        m = mod_ref[0]
        h = h_ref[...]
    a = (h * (1.0 + m[4:5]) + m[3:4]).astype(BF16)
    hidden = wg_ref.shape[1]
    for lo in range(0, hidden, FFN_COLS):
        g = _dot(a, wg_ref[:, lo:lo + FFN_COLS])
        u = _dot(a, wu_ref[:, lo:lo + FFN_COLS])
        hid_ref[:, lo:lo + FFN_COLS] = (_silu(g) * u).astype(BF16)
    f = _dot(hid_ref[...], wd_ref[...])
    res = ALPHA * h + m[5:6] * f
    out_ref[...] = _layer_norm(res, lng_ref[...], lnb_ref[...])


def _ffn(acts, mods, consts, tiles_per_batch, gla_prologue):
    n, d = acts[0].shape
    tm = FFN_TILE
    hidden = consts[-5].shape[1]
    row = lambda i: (i, 0)
    mod_spec = pl.BlockSpec((1,) + mods.shape[1:], lambda i: (i // tiles_per_batch, 0, 0))
    in_specs = ([pl.BlockSpec((tm, t.shape[1]), row) for t in acts] + [mod_spec]
                + [_const_spec(t.shape) for t in consts])
    return pl.pallas_call(
        functools.partial(_ffn_kernel, gla_prologue=gla_prologue),
        grid=(n // tm,),
        in_specs=in_specs,
        out_specs=pl.BlockSpec((tm, d), row),
        out_shape=jax.ShapeDtypeStruct((n, d), F32),
        scratch_shapes=[pltpu.VMEM((tm, hidden), BF16)],
        compiler_params=_params("parallel"),
        name="gla_out_ffn" if gla_prologue else "swiglu_ffn",
    )(*acts, mods, *consts)


def _gmlp_kernel(h_ref, mod_ref, win_ref, vg_ref, vb_ref, ws_ref, bs_ref, wout_ref, lng_ref, lnb_ref,
                 o_ref, u_scr, v_scr, gat_scr):
    m = mod_ref[0]
    h = h_ref[...]
    tm = h.shape[0]
    width = u_scr.shape[1]
    a = (h * (1.0 + m[1:2]) + m[0:1]).astype(BF16)
    for lo in range(0, 2 * width, GMLP_COLS):
        z = _dot(a, win_ref[:, lo:lo + GMLP_COLS])
        z = 0.5 * z * (1.0 + lax.erf(z * (0.5 ** 0.5)))
        if lo < width:
            u_scr[:, lo:lo + GMLP_COLS] = z
        else:
            v_scr[:, lo - width:lo - width + GMLP_COLS] = z
    v = v_scr[...]
    mu = jnp.mean(v, axis=-1, keepdims=True)
    vc = v - mu
    rstd = lax.rsqrt(jnp.mean(vc * vc, axis=-1, keepdims=True) + 1e-5)
    gw = width // GM_GROUPS
    for t in range(tm // GM_CHUNK):
        rows = slice(t * GM_CHUNK, (t + 1) * GM_CHUNK)
        for gi in range(GM_GROUPS):
            cols = slice(gi * gw, (gi + 1) * gw)
            vn = (v_scr[rows, cols] - mu[rows]) * rstd[rows] * vg_ref[:, cols] + vb_ref[:, cols]
            s = _dot(ws_ref[gi], vn.astype(BF16)) + jnp.concatenate([bs_ref[gi]] * (gw // LANES), axis=1)
            gat_scr[rows, cols] = (u_scr[rows, cols] * s).astype(BF16)
    y = _dot(gat_scr[...], wout_ref[...])
    res = ALPHA * h + m[2:3] * y
    o_ref[...] = _layer_norm(res, lng_ref[...], lnb_ref[...])


def _gmlp(h2d, mods, consts, tiles_per_batch):
    n, d = h2d.shape
    tm = GMLP_TILE
    width = consts[5].shape[0]
    row = lambda i: (i, 0)
    return pl.pallas_call(
        _gmlp_kernel,
        grid=(n // tm,),
        in_specs=[
            pl.BlockSpec((tm, d), row),
            pl.BlockSpec((1,) + mods.shape[1:], lambda i: (i // tiles_per_batch, 0, 0)),
        ] + [_const_spec(t.shape) for t in consts],
        out_specs=pl.BlockSpec((tm, d), row),
        out_shape=jax.ShapeDtypeStruct((n, d), F32),
        scratch_shapes=[pltpu.VMEM((tm, width), F32), pltpu.VMEM((tm, width), F32),
                        pltpu.VMEM((tm, width), BF16)],
        compiler_params=_params("parallel"),
        name="gmlp_mixer",
    )(h2d, mods, *consts)


def kernel(x, c, ctx, c_ctx, mod_w, mod_b, ln_g, ln_b, gla_w_in, gla_w_decay, gla_b_decay, gla_norm_g,
           gla_w_out, gm_w_in, gm_ln_g, gm_ln_b, gm_w_s, gm_b_s, gm_w_out, ffn_w_gate, ffn_w_up, ffn_w_down):
    bsz, l, d = x.shape
    lc = ctx.shape[1]
    n = bsz * l
    assert bsz + 1 <= COND_ROWS
    dk = gla_w_decay.shape[-1]
    dv = gla_w_out.shape[1]
    q_scale = (dk // GLA_HEADS) ** -0.5
    vec = lambda t: t.reshape(1, -1)

    cond = jnp.concatenate([c, c_ctx[None], jnp.zeros((COND_ROWS - bsz - 1, d), F32)], axis=0)
    mods = _adaln(cond, mod_w, mod_b).reshape(DEPTH, COND_ROWS, 6, d)

    w_in = gla_w_in[0]
    o_a = dk + dv
    o_q = o_a + 2 * GLA_RANK
    w_k, w_v, w_a = w_in[:, :dk], w_in[:, dk:o_a], w_in[:, o_a:o_q]
    w_q, w_r = w_in[:, o_q:o_q + dk], w_in[:, o_q + dk:]
    w_a = jnp.pad(w_a, ((0, 0), (0, LANES - 2 * GLA_RANK)))
    w_lat = jnp.concatenate([w_k, w_q, w_a, w_v, w_r], axis=1).astype(BF16)
    w_ctx = jnp.concatenate([w_k, w_a, w_v], axis=1).astype(BF16)
    wdec = jnp.zeros((2, LANES, dk), F32)
    wdec = wdec.at[0, :GLA_RANK].set(gla_w_decay[0, 0]).at[1, GLA_RANK:2 * GLA_RANK].set(gla_w_decay[0, 1])
    wdec = wdec.astype(BF16)
    bdec = gla_b_decay[0].reshape(2, 1, dk)

    tpb = l // PROJ_TILE
    (v, gate, qin_f, kin_f, kst_f, dec_f, qin_b, kin_b, kst_b, dec_b) = _gla_proj(
        x.reshape(n, d), mods[0], w_lat, wdec, bdec, lambda i: i // tpb, True, PROJ_TILE, dk, dv, q_scale)
    (vc, kstc_f, decc_f, kstc_b, decc_b) = _gla_proj(
        ctx.reshape(bsz * lc, d), mods[0], w_ctx, wdec, bdec, lambda i: bsz, False, lc, dk, dv, q_scale)
    b3 = lambda t: t.reshape(bsz, l, -1)
    v3, vc3 = b3(v), vc.reshape(bsz, lc, dv)
    o_b = _gla_scan(b3(qin_b), b3(kin_b), kst_b, v3, dec_b, kstc_b, vc3, decc_b, True)
    o = _gla_scan(b3(qin_f), b3(kin_f), kst_f, v3, dec_f, kstc_f, vc3, decc_f, False, prev=o_b)

    ffn_consts = lambda i: (ffn_w_gate[i].astype(BF16), ffn_w_up[i].astype(BF16), ffn_w_down[i].astype(BF16),
                            vec(ln_g[i, 1]), vec(ln_b[i, 1]))
    gla_consts = (vec(gla_norm_g[0]), gla_w_out[0].astype(BF16), vec(ln_g[0, 0]), vec(ln_b[0, 0]))
    tpf = l // FFN_TILE
    h = _ffn((o.reshape(n, dv), gate, x.reshape(n, d)), mods[0], gla_consts + ffn_consts(0), tpf, True)

    bs = jnp.broadcast_to(gm_b_s[0].T[:, :, None], (GM_GROUPS, GM_CHUNK, LANES))
    gm_consts = (gm_w_in[0].astype(BF16), vec(gm_ln_g[0]), vec(gm_ln_b[0]), gm_w_s[0].astype(BF16), bs,
                 gm_w_out[0].astype(BF16), vec(ln_g[1, 0]), vec(ln_b[1, 0]))
    h = _gmlp(h, mods[1], gm_consts, l // GMLP_TILE)
    h = _ffn((h,), mods[1], ffn_consts(1), tpf, False)
    return h.reshape(bsz, l, d)
```

```python
import functools

import jax
import jax.numpy as jnp
from jax import lax
from jax.experimental import pallas as pl
from jax.experimental.pallas import tpu as pltpu

F32 = jnp.float32
BF16 = jnp.bfloat16

DEPTH = 2
ALPHA = (2 * DEPTH) ** 0.25
GLA_HEADS = 4
GLA_RANK = 16
GLA_GATE_NORM = 16.0
GLA_CHUNK = 64
GM_GROUPS = 4
GM_CHUNK = 128
COND_ROWS = 8
LANES = 128
VMEM_LIMIT = 56 * 1024 * 1024

PROJ_TILE = 512
SCAN_BLOCK = 256
FFN_TILE = 512
FFN_COLS = 256
GMLP_TILE = 512
GMLP_COLS = 512


def _dot(a, b):
    return jnp.dot(a, b, preferred_element_type=F32)


def _layer_norm(x, g, b, eps=1e-5):
    mu = jnp.mean(x, axis=-1, keepdims=True)
    xc = x - mu
    var = jnp.mean(xc * xc, axis=-1, keepdims=True)
    return xc * lax.rsqrt(var + eps) * g + b


def _silu(x):
    return x * jax.nn.sigmoid(x)


def _params(*sem):
    return pltpu.CompilerParams(dimension_semantics=sem, vmem_limit_bytes=VMEM_LIMIT)


def _const_spec(shape):
    nd = len(shape)
    return pl.BlockSpec(shape, lambda *_: (0,) * nd, pipeline_mode=pl.Buffered(1))


def _adaln_kernel(cond_ref, w_ref, b_ref, o_ref):
    s = _silu(cond_ref[...]).astype(BF16)
    o_ref[0] = _dot(s, w_ref[0].astype(BF16)) + b_ref[0]


def _adaln(cond, mod_w, mod_b):
    depth, d, n = mod_w.shape
    tn = n // 4
    return pl.pallas_call(
        _adaln_kernel,
        grid=(depth, n // tn),
        in_specs=[
            pl.BlockSpec((COND_ROWS, d), lambda i, j: (0, 0)),
            pl.BlockSpec((1, d, tn), lambda i, j: (i, 0, j)),
            pl.BlockSpec((1, 1, tn), lambda i, j: (i, 0, j)),
        ],
        out_specs=pl.BlockSpec((1, COND_ROWS, tn), lambda i, j: (i, 0, j)),
        out_shape=jax.ShapeDtypeStruct((depth, COND_ROWS, n), F32),
        compiler_params=_params("arbitrary", "arbitrary"),
        name="adaln",
    )(cond, mod_w, mod_b.reshape(depth, 1, n))


def _log_sigmoid(z):
    return jnp.minimum(z, 0.0) - jnp.log(1.0 + jnp.exp(-jnp.abs(z)))


def _chunk_tri(n, reverse):
    row = lax.broadcasted_iota(jnp.int32, (n, n), 0)
    col = lax.broadcasted_iota(jnp.int32, (n, n), 1)
    shift = GLA_CHUNK.bit_length() - 1
    same = jnp.right_shift(row, shift) == jnp.right_shift(col, shift)
    return jnp.where(same & ((col >= row) if reverse else (col <= row)), 1.0, 0.0).astype(BF16)


def _gla_proj_kernel(x_ref, mod_ref, w_ref, wdec_ref, bdec_ref, *refs, latent, dk, dv, q_scale, n_cast):
    for src, dst in zip(refs[:n_cast], refs[len(refs) - n_cast:]):
        dst[...] = src[...].astype(BF16)
    out_refs = refs[n_cast:len(refs) - n_cast]
    m = mod_ref[0]
    a = (x_ref[...] * (1.0 + m[1:2]) + m[0:1]).astype(BF16)
    tm = a.shape[0]
    nblk = tm // SCAN_BLOCK
    ncb = SCAN_BLOCK // GLA_CHUNK
    o_a = 2 * dk if latent else dk
    o_v = o_a + LANES
    if latent:
        v_ref, gate_ref = out_refs[:2]
        dir_refs = (out_refs[2:6], out_refs[6:10])
    else:
        v_ref = out_refs[0]
        dir_refs = ((None, None) + tuple(out_refs[1:3]), (None, None) + tuple(out_refs[3:5]))
    k = _dot(a, w_ref[:, 0:dk])
    if latent:
        q = _dot(a, w_ref[:, dk:2 * dk]) * q_scale
    a_lr = _dot(a, w_ref[:, o_a:o_a + LANES]).astype(BF16)
    z = [_dot(a_lr, wdec_ref[rev]) + bdec_ref[rev] for rev in (0, 1)]
    v_ref[...] = _dot(a, w_ref[:, o_v:o_v + dv]).astype(BF16)
    g_parts = []
    for rev in (0, 1):
        g = _log_sigmoid(z[rev]) * (1.0 / GLA_GATE_NORM)
        g_hi = g.astype(BF16)
        g_parts.append((g_hi, (g - g_hi.astype(F32)).astype(BF16)))
    tris = (_chunk_tri(SCAN_BLOCK, False), _chunk_tri(SCAN_BLOCK, True))
    b = {}
    for rev in (0, 1):
        for blk in range(nblk):
            rows = slice(blk * SCAN_BLOCK, (blk + 1) * SCAN_BLOCK)
            b[rev, blk] = _dot(tris[rev], g_parts[rev][0][rows]) + _dot(tris[rev], g_parts[rev][1][rows])
    if latent:
        gate_ref[...] = _silu(_dot(a, w_ref[:, o_v + dv:o_v + 2 * dv])).astype(BF16)
    for rev in (0, 1):
        qin_ref, kin_ref, kst_ref, dec_ref = dir_refs[rev]
        for blk in range(nblk):
            rows = slice(blk * SCAN_BLOCK, (blk + 1) * SCAN_BLOCK)
            bb = b[rev, blk]
            if latent:
                qin_ref[rows, :] = (q[rows] * jnp.exp(bb)).astype(BF16)
                kin_ref[rows, :] = (k[rows] * jnp.exp(-bb)).astype(BF16)
            ends = []
            for c in range(ncb):
                lo = c * GLA_CHUNK
                b_c = bb[lo:lo + GLA_CHUNK]
                b_end = b_c[0:1] if rev else b_c[GLA_CHUNK - 1:GLA_CHUNK]
                kst = k[blk * SCAN_BLOCK + lo:blk * SCAN_BLOCK + lo + GLA_CHUNK] * jnp.exp(b_end - b_c)
                kst_ref[blk * ncb + c] = kst.T.astype(BF16)
                ends.append(b_end)
            dec_ref[blk] = jnp.exp(jnp.concatenate(ends, axis=0))


def _gla_proj(x2d, mods, w, wdec, bdec, mod_row, latent, tm, dk, dv, q_scale, cast=()):
    n, d = x2d.shape
    steps = n // tm
    ncb = SCAN_BLOCK // GLA_CHUNK
    row = lambda i: (i, 0)
    row3 = lambda i: (i, 0, 0)
    slabs = [t.reshape(steps, t.shape[0] // steps, t.shape[1]) for t in cast]
    slab_specs = [pl.BlockSpec((1,) + t.shape[1:], row3) for t in slabs]
    tok = lambda c: (jax.ShapeDtypeStruct((n, c), BF16), pl.BlockSpec((tm, c), row))
    kst = (jax.ShapeDtypeStruct((n // GLA_CHUNK, dk, GLA_CHUNK), BF16),
           pl.BlockSpec((tm // GLA_CHUNK, dk, GLA_CHUNK), row3))
    dec = (jax.ShapeDtypeStruct((n // SCAN_BLOCK, ncb, dk), F32),
           pl.BlockSpec((tm // SCAN_BLOCK, ncb, dk), row3))
    per_dir = [tok(dk), tok(dk), kst, dec] if latent else [kst, dec]
    outs = ([tok(dv), tok(dv)] if latent else [tok(dv)]) + per_dir + per_dir
    res = pl.pallas_call(
        functools.partial(_gla_proj_kernel, latent=latent, dk=dk, dv=dv, q_scale=q_scale, n_cast=len(cast)),
        grid=(steps,),
        in_specs=[
            pl.BlockSpec((tm, d), row),
            pl.BlockSpec((1,) + mods.shape[1:], lambda i: (mod_row(i), 0, 0)),
            _const_spec(w.shape), _const_spec(wdec.shape), _const_spec(bdec.shape),
        ] + slab_specs,
        out_specs=[o[1] for o in outs] + slab_specs,
        out_shape=[o[0] for o in outs] + [jax.ShapeDtypeStruct(t.shape, BF16) for t in slabs],
        compiler_params=_params("parallel"),
        name="gla_in_proj" if latent else "gla_ctx_proj",
    )(x2d, mods, w, wdec, bdec, *slabs)
    n_out = len(outs)
    return res[:n_out], [r.reshape(t.shape) for r, t in zip(res[n_out:], cast)]


def _decay_columns(dec_row, dvh):
    dkh = dec_row.shape[1]
    dcol = jnp.broadcast_to(dec_row, (dkh, dkh)).T
    return jnp.concatenate([dcol] * (dvh // dkh), axis=1)


def _gla_scan_kernel(*refs, reverse, add_prev):
    qin_ref, kin_ref, kst_ref, v_ref, dec_ref, kstc_ref, vc_ref, decc_ref = refs[:8]
    if add_prev:
        prev_ref, o_ref, s_ref = refs[8:]
    else:
        o_ref, s_ref = refs[8:]
    dkh, dvh = s_ref.shape[1:]
    heads = range(GLA_HEADS)
    ks = [slice(h * dkh, (h + 1) * dkh) for h in heads]
    vs = [slice(h * dvh, (h + 1) * dvh) for h in heads]

    def order(nc):
        return list(range(nc - 1, -1, -1) if reverse else range(nc))

    def rows(c):
        return slice(c * GLA_CHUNK, (c + 1) * GLA_CHUNK)

    @pl.when(pl.program_id(1) == 0)
    def _():
        chunks = order(vc_ref.shape[1] // GLA_CHUNK)
        upd = {(c, h): _dot(kstc_ref[c, ks[h], :], vc_ref[0, rows(c), vs[h]]) for c in chunks for h in heads}
        for h in heads:
            s = jnp.zeros((dkh, dvh), F32)
            for c in chunks:
                s = s * _decay_columns(decc_ref[0, c:c + 1, ks[h]], dvh) + upd[c, h]
            s_ref[h] = s

    chunks = order(v_ref.shape[1] // GLA_CHUNK)
    pairs = [(c, h) for c in chunks for h in heads]
    r64 = lax.broadcasted_iota(jnp.int32, (GLA_CHUNK, GLA_CHUNK), 0)
    c64 = lax.broadcasted_iota(jnp.int32, (GLA_CHUNK, GLA_CHUNK), 1)
    mask = (c64 >= r64) if reverse else (c64 <= r64)
    att = {(c, h): lax.dot_general(qin_ref[0, rows(c), ks[h]], kin_ref[0, rows(c), ks[h]],
                                   (((1,), (1,)), ((), ())), preferred_element_type=F32) for c, h in pairs}
    upd = {(c, h): _dot(kst_ref[c, ks[h], :], v_ref[0, rows(c), vs[h]]) for c, h in pairs}
    dcol = {(c, h): _decay_columns(dec_ref[0, c:c + 1, ks[h]], dvh) for c, h in pairs}
    att = {p: jnp.where(mask, att[p], 0.0).astype(BF16) for p in pairs}
    intra = {(c, h): _dot(att[c, h], v_ref[0, rows(c), vs[h]]) for c, h in pairs}
    state = [s_ref[h] for h in heads]
    for c in chunks:
        outs = []
        for h in heads:
            outs.append(intra[c, h] + _dot(qin_ref[0, rows(c), ks[h]], state[h].astype(BF16)))
            state[h] = state[h] * dcol[c, h] + upd[c, h]
        o = jnp.concatenate(outs, axis=1)
        if add_prev:
            o = o + prev_ref[0, rows(c), :].astype(F32)
        o_ref[0, rows(c), :] = o.astype(o_ref.dtype)
    for h in heads:
        s_ref[h] = state[h]


def _gla_scan(qin, kin, kst, v, dec, kstc, vc, decc, reverse, prev=None):
    bsz, l, dk = qin.shape
    dv = v.shape[2]
    lc = vc.shape[1]
    tb = SCAN_BLOCK
    nblk = l // tb
    ncb = tb // GLA_CHUNK

    def pos(j):
        return (nblk - 1 - j) if reverse else j

    blk = lambda b, j: (b, pos(j), 0)
    flat = lambda b, j: (b * nblk + pos(j), 0, 0)
    ctx = lambda b, j: (b, 0, 0)
    in_specs = [
        pl.BlockSpec((1, tb, dk), blk), pl.BlockSpec((1, tb, dk), blk),
        pl.BlockSpec((ncb, dk, GLA_CHUNK), flat), pl.BlockSpec((1, tb, dv), blk),
        pl.BlockSpec((1, ncb, dk), flat),
        pl.BlockSpec((lc // GLA_CHUNK, dk, GLA_CHUNK), ctx), pl.BlockSpec((1, lc, dv), ctx),
        pl.BlockSpec((1, lc // GLA_CHUNK, dk), ctx),
    ]
    args = [qin, kin, kst, v, dec, kstc, vc, decc]
    if prev is not None:
        in_specs.append(pl.BlockSpec((1, tb, dv), blk))
        args.append(prev)
    return pl.pallas_call(
        functools.partial(_gla_scan_kernel, reverse=reverse, add_prev=prev is not None),
        grid=(bsz, nblk),
        in_specs=in_specs,
        out_specs=pl.BlockSpec((1, tb, dv), blk),
        out_shape=jax.ShapeDtypeStruct((bsz, l, dv), BF16),
        scratch_shapes=[pltpu.VMEM((GLA_HEADS, dk // GLA_HEADS, dv // GLA_HEADS), F32)],
        compiler_params=_params("arbitrary", "arbitrary"),
        name="gla_scan_bwd" if reverse else "gla_scan_fwd",
    )(*args)


def _ffn_kernel(*refs, gla_prologue):
    if gla_prologue:
        (o_ref, gate_ref, x_ref, mod_ref, ng_ref, wout_ref, lng0_ref, lnb0_ref,
         wg_ref, wu_ref, wd_ref, lng_ref, lnb_ref, out_ref, hid_ref) = refs
        m = mod_ref[0]
        o = o_ref[...].astype(F32)
        dvh = o.shape[1] // GLA_HEADS
        normed = []
        for hd in range(GLA_HEADS):
            o_h = o[:, hd * dvh:(hd + 1) * dvh]
            ms = jnp.mean(o_h * o_h, axis=-1, keepdims=True)
            normed.append(o_h * lax.rsqrt(ms + 1e-6) * ng_ref[...])
        gated = (jnp.concatenate(normed, axis=1) * gate_ref[...].astype(F32)).astype(BF16)
        y = _dot(gated, wout_ref[...])
        h = _layer_norm(ALPHA * x_ref[...] + m[2:3] * y, lng0_ref[...], lnb0_ref[...])
    else:
        h_ref, mod_ref, wg_ref, wu_ref, wd_ref, lng_ref, lnb_ref, out_ref, hid_ref = refs
        m = mod_ref[0]
        h = h_ref[...]
    a = (h * (1.0 + m[4:5]) + m[3:4]).astype(BF16)
    hidden = wg_ref.shape[1]
    for lo in range(0, hidden, FFN_COLS):
        g = _dot(a, wg_ref[:, lo:lo + FFN_COLS])
        u = _dot(a, wu_ref[:, lo:lo + FFN_COLS])
        hid_ref[:, lo:lo + FFN_COLS] = (_silu(g) * u).astype(BF16)
    f = _dot(hid_ref[...], wd_ref[...])
    res = ALPHA * h + m[5:6] * f
    out_ref[...] = _layer_norm(res, lng_ref[...], lnb_ref[...])


def _ffn(acts, mods, consts, tiles_per_batch, gla_prologue):
    n, d = acts[0].shape
    tm = FFN_TILE
    hidden = consts[-5].shape[1]
    row = lambda i: (i, 0)
    mod_spec = pl.BlockSpec((1,) + mods.shape[1:], lambda i: (i // tiles_per_batch, 0, 0))
    in_specs = ([pl.BlockSpec((tm, t.shape[1]), row) for t in acts] + [mod_spec]
                + [_const_spec(t.shape) for t in consts])
    return pl.pallas_call(
        functools.partial(_ffn_kernel, gla_prologue=gla_prologue),
        grid=(n // tm,),
        in_specs=in_specs,
        out_specs=pl.BlockSpec((tm, d), row),
        out_shape=jax.ShapeDtypeStruct((n, d), F32),
        scratch_shapes=[pltpu.VMEM((tm, hidden), BF16)],
        compiler_params=_params("parallel"),
        name="gla_out_ffn" if gla_prologue else "swiglu_ffn",
    )(*acts, mods, *consts)


def _gmlp_kernel(h_ref, mod_ref, win_ref, vg_ref, vb_ref, ws_ref, bs_ref, wout_ref, lng_ref, lnb_ref,
                 o_ref, v_scr):
    m = mod_ref[0]
    h = h_ref[...]
    tm = h.shape[0]
    width = v_scr.shape[1]
    gw = width // GM_GROUPS
    a = (h * (1.0 + m[1:2]) + m[0:1]).astype(BF16)

    def gelu(z):
        return 0.5 * z * (1.0 + lax.erf(z * (0.5 ** 0.5)))

    def lane_fold(t):
        acc = t[:, 0:LANES]
        for j in range(1, t.shape[1] // LANES):
            acc = acc + t[:, j * LANES:(j + 1) * LANES]
        return acc

    part = None
    for lo in range(0, width, GMLP_COLS):
        z = gelu(_dot(a, win_ref[:, width + lo:width + lo + GMLP_COLS]))
        v_scr[:, lo:lo + GMLP_COLS] = z
        part = lane_fold(z) if part is None else part + lane_fold(z)
    mu = jnp.sum(part, axis=-1, keepdims=True) * (1.0 / width)
    u_next = _dot(a, win_ref[:, 0:gw])
    part = None
    for lo in range(0, width, GMLP_COLS):
        vc = v_scr[:, lo:lo + GMLP_COLS] - mu
        part = lane_fold(vc * vc) if part is None else part + lane_fold(vc * vc)
    rstd = lax.rsqrt(jnp.sum(part, axis=-1, keepdims=True) * (1.0 / width) + 1e-5)
    y = None
    for gi in range(GM_GROUPS):
        cols = slice(gi * gw, (gi + 1) * gw)
        u = gelu(u_next)
        if gi + 1 < GM_GROUPS:
            u_next = _dot(a, win_ref[:, (gi + 1) * gw:(gi + 2) * gw])
        bias = jnp.concatenate([bs_ref[gi]] * (gw // LANES), axis=1)
        gated = []
        for t in range(tm // GM_CHUNK):
            rows = slice(t * GM_CHUNK, (t + 1) * GM_CHUNK)
            vn = (v_scr[rows, cols] - mu[rows]) * rstd[rows] * vg_ref[:, cols] + vb_ref[:, cols]
            s = _dot(ws_ref[gi], vn.astype(BF16)) + bias
            gated.append((u[rows] * s).astype(BF16))
        y_g = _dot(jnp.concatenate(gated, axis=0), wout_ref[cols, :])
        y = y_g if y is None else y + y_g
    res = ALPHA * h + m[2:3] * y
    o_ref[...] = _layer_norm(res, lng_ref[...], lnb_ref[...])


def _gmlp(h2d, mods, consts, tiles_per_batch):
    n, d = h2d.shape
    tm = GMLP_TILE
    width = consts[5].shape[0]
    row = lambda i: (i, 0)
    return pl.pallas_call(
        _gmlp_kernel,
        grid=(n // tm,),
        in_specs=[
            pl.BlockSpec((tm, d), row),
            pl.BlockSpec((1,) + mods.shape[1:], lambda i: (i // tiles_per_batch, 0, 0)),
        ] + [_const_spec(t.shape) for t in consts],
        out_specs=pl.BlockSpec((tm, d), row),
        out_shape=jax.ShapeDtypeStruct((n, d), F32),
        scratch_shapes=[pltpu.VMEM((tm, width), F32)],
        compiler_params=_params("parallel"),
        name="gmlp_mixer",
    )(h2d, mods, *consts)


def kernel(x, c, ctx, c_ctx, mod_w, mod_b, ln_g, ln_b, gla_w_in, gla_w_decay, gla_b_decay, gla_norm_g,
           gla_w_out, gm_w_in, gm_ln_g, gm_ln_b, gm_w_s, gm_b_s, gm_w_out, ffn_w_gate, ffn_w_up, ffn_w_down):
    bsz, l, d = x.shape
    lc = ctx.shape[1]
    n = bsz * l
    assert bsz + 1 <= COND_ROWS
    dk = gla_w_decay.shape[-1]
    dv = gla_w_out.shape[1]
    q_scale = (dk // GLA_HEADS) ** -0.5
    vec = lambda t: t.reshape(1, -1)

    cond = jnp.concatenate([c, c_ctx[None], jnp.zeros((COND_ROWS - bsz - 1, d), F32)], axis=0)
    mods = _adaln(cond, mod_w, mod_b).reshape(DEPTH, COND_ROWS, 6, d)

    w_in = gla_w_in[0]
    o_a = dk + dv
    o_q = o_a + 2 * GLA_RANK
    w_k, w_v, w_a = w_in[:, :dk], w_in[:, dk:o_a], w_in[:, o_a:o_q]
    w_q, w_r = w_in[:, o_q:o_q + dk], w_in[:, o_q + dk:]
    w_a = jnp.pad(w_a, ((0, 0), (0, LANES - 2 * GLA_RANK)))
    w_lat = jnp.concatenate([w_k, w_q, w_a, w_v, w_r], axis=1).astype(BF16)
    w_ctx = jnp.concatenate([w_k, w_a, w_v], axis=1).astype(BF16)
    wdec = jnp.zeros((2, LANES, dk), F32)
    wdec = wdec.at[0, :GLA_RANK].set(gla_w_decay[0, 0]).at[1, GLA_RANK:2 * GLA_RANK].set(gla_w_decay[0, 1])
    wdec = wdec.astype(BF16)
    bdec = gla_b_decay[0].reshape(2, 1, dk)

    tpb = l // PROJ_TILE
    later = (gla_w_out[0], ffn_w_gate[0], ffn_w_up[0], ffn_w_down[0], gm_w_in[0], gm_w_out[0],
             ffn_w_gate[1], ffn_w_up[1], ffn_w_down[1])
    ((v, gate, qin_f, kin_f, kst_f, dec_f, qin_b, kin_b, kst_b, dec_b),
     (w_out, wg0, wu0, wd0, gm_in, gm_out, wg1, wu1, wd1)) = _gla_proj(
        x.reshape(n, d), mods[0], w_lat, wdec, bdec, lambda i: i // tpb, True, PROJ_TILE, dk, dv, q_scale,
        cast=later)
    (vc, kstc_f, decc_f, kstc_b, decc_b), _ = _gla_proj(
        ctx.reshape(bsz * lc, d), mods[0], w_ctx, wdec, bdec, lambda i: bsz, False, lc, dk, dv, q_scale)
    b3 = lambda t: t.reshape(bsz, l, -1)
    v3, vc3 = b3(v), vc.reshape(bsz, lc, dv)
    o_b = _gla_scan(b3(qin_b), b3(kin_b), kst_b, v3, dec_b, kstc_b, vc3, decc_b, True)
    o = _gla_scan(b3(qin_f), b3(kin_f), kst_f, v3, dec_f, kstc_f, vc3, decc_f, False, prev=o_b)

    ffn_w = ((wg0, wu0, wd0), (wg1, wu1, wd1))
    ffn_consts = lambda i: ffn_w[i] + (vec(ln_g[i, 1]), vec(ln_b[i, 1]))
    gla_consts = (vec(gla_norm_g[0]), w_out, vec(ln_g[0, 0]), vec(ln_b[0, 0]))
    tpf = l // FFN_TILE
    h = _ffn((o.reshape(n, dv), gate, x.reshape(n, d)), mods[0], gla_consts + ffn_consts(0), tpf, True)

    bs = jnp.broadcast_to(gm_b_s[0].T[:, :, None], (GM_GROUPS, GM_CHUNK, LANES))
    gm_consts = (gm_in, vec(gm_ln_g[0]), vec(gm_ln_b[0]), gm_w_s[0].astype(BF16), bs,
                 gm_out, vec(ln_g[1, 0]), vec(ln_b[1, 0]))
    h = _gmlp(h, mods[1], gm_consts, l // GMLP_TILE)
    h = _ffn((h,), mods[1], ffn_consts(1), tpf, False)
    return h.reshape(bsz, l, d)
```

```python
import functools

import jax
import jax.numpy as jnp
from jax import lax
from jax.experimental import pallas as pl
from jax.experimental.pallas import tpu as pltpu

F32 = jnp.float32
BF16 = jnp.bfloat16

DEPTH = 2
ALPHA = (2 * DEPTH) ** 0.25
GLA_HEADS = 4
GLA_RANK = 16
GLA_GATE_NORM = 16.0
GLA_CHUNK = 64
GM_GROUPS = 4
GM_CHUNK = 128
COND_ROWS = 8
LANES = 128
VMEM_LIMIT = 56 * 1024 * 1024

PROJ_TILE = 512
SCAN_BLOCK = 256
FFN_TILE = 1024
FFN_SUB = 512
FFN_HEAD_AT = (3, 6)
FFN_TAIL_PIECES = 8
FFN_COLS = 256
GMLP_TILE = 512
GMLP_COLS = 512


def _dot(a, b):
    return jnp.dot(a, b, preferred_element_type=F32)


def _layer_norm(x, g, b, eps=1e-5):
    mu = jnp.mean(x, axis=-1, keepdims=True)
    xc = x - mu
    var = jnp.mean(xc * xc, axis=-1, keepdims=True)
    return xc * lax.rsqrt(var + eps) * g + b


def _silu(x):
    return x * jax.nn.sigmoid(x)


def _lane_fold(t):
    acc = t[:, 0:LANES]
    for j in range(1, t.shape[1] // LANES):
        acc = acc + t[:, j * LANES:(j + 1) * LANES]
    return acc


def _order_after(x, dep):
    bits = pltpu.bitcast(dep[0:x.shape[0], 0:LANES], jnp.uint32)
    half = jnp.uint32(16)
    zero = pltpu.bitcast(lax.shift_right_logical(lax.shift_right_logical(bits, half), half), F32)
    return jnp.concatenate([x[:, 0:LANES] + zero, x[:, LANES:]], axis=1)


def _params(*sem):
    return pltpu.CompilerParams(dimension_semantics=sem, vmem_limit_bytes=VMEM_LIMIT)


def _const_spec(shape):
    nd = len(shape)
    return pl.BlockSpec(shape, lambda *_: (0,) * nd, pipeline_mode=pl.Buffered(1))


def _adaln_kernel(cond_ref, w_ref, b_ref, o_ref):
    s = _silu(cond_ref[...]).astype(BF16)
    o_ref[0] = _dot(s, w_ref[0].astype(BF16)) + b_ref[0]


def _adaln(cond, mod_w, mod_b):
    depth, d, n = mod_w.shape
    tn = n // 4
    return pl.pallas_call(
        _adaln_kernel,
        grid=(depth, n // tn),
        in_specs=[
            pl.BlockSpec((COND_ROWS, d), lambda i, j: (0, 0)),
            pl.BlockSpec((1, d, tn), lambda i, j: (i, 0, j)),
            pl.BlockSpec((1, 1, tn), lambda i, j: (i, 0, j)),
        ],
        out_specs=pl.BlockSpec((1, COND_ROWS, tn), lambda i, j: (i, 0, j)),
        out_shape=jax.ShapeDtypeStruct((depth, COND_ROWS, n), F32),
        compiler_params=_params("arbitrary", "arbitrary"),
        name="adaln",
    )(cond, mod_w, mod_b.reshape(depth, 1, n))


def _log_sigmoid(z):
    return jnp.minimum(z, 0.0) - jnp.log(1.0 + jnp.exp(-jnp.abs(z)))


def _chunk_tri(n, reverse):
    row = lax.broadcasted_iota(jnp.int32, (n, n), 0)
    col = lax.broadcasted_iota(jnp.int32, (n, n), 1)
    shift = GLA_CHUNK.bit_length() - 1
    same = jnp.right_shift(row, shift) == jnp.right_shift(col, shift)
    return jnp.where(same & ((col >= row) if reverse else (col <= row)), 1.0, 0.0).astype(BF16)


def _gla_proj_kernel(x_ref, mod_ref, w_ref, wdec_ref, bdec_ref, *refs, latent, dk, dv, q_scale, cast_layers):
    n_src, n_dst = len(cast_layers), sum(cast_layers)
    dsts = iter(refs[len(refs) - n_dst:])
    for src, layers in zip(refs[:n_src], cast_layers):
        for layer in range(layers):
            next(dsts)[0] = src[layer, 0].astype(BF16)
    out_refs = refs[n_src:len(refs) - n_dst]
    m = mod_ref[0]
    a = (x_ref[...] * (1.0 + m[1:2]) + m[0:1]).astype(BF16)
    tm = a.shape[0]
    nblk = tm // SCAN_BLOCK
    ncb = SCAN_BLOCK // GLA_CHUNK
    o_a = 2 * dk if latent else dk
    o_v = o_a + LANES
    if latent:
        v_ref, gate_ref = out_refs[:2]
        dir_refs = (out_refs[2:6], out_refs[6:10])
    else:
        v_ref = out_refs[0]
        dir_refs = ((None, None) + tuple(out_refs[1:3]), (None, None) + tuple(out_refs[3:5]))
    k = _dot(a, w_ref[:, 0:dk])
    if latent:
        q = _dot(a, w_ref[:, dk:2 * dk]) * q_scale
    a_lr = _dot(a, w_ref[:, o_a:o_a + LANES]).astype(BF16)
    z = [_dot(a_lr, wdec_ref[rev]) + bdec_ref[rev] for rev in (0, 1)]
    v_ref[...] = _dot(a, w_ref[:, o_v:o_v + dv]).astype(BF16)
    g_parts = []
    for rev in (0, 1):
        g = _log_sigmoid(z[rev]) * (1.0 / GLA_GATE_NORM)
        g_hi = g.astype(BF16)
        g_parts.append((g_hi, (g - g_hi.astype(F32)).astype(BF16)))
    tris = (_chunk_tri(SCAN_BLOCK, False), _chunk_tri(SCAN_BLOCK, True))
    b = {}
    for rev in (0, 1):
        for blk in range(nblk):
            rows = slice(blk * SCAN_BLOCK, (blk + 1) * SCAN_BLOCK)
            b[rev, blk] = _dot(tris[rev], g_parts[rev][0][rows]) + _dot(tris[rev], g_parts[rev][1][rows])
    if latent:
        gate_ref[...] = _silu(_dot(a, w_ref[:, o_v + dv:o_v + 2 * dv])).astype(BF16)
    for rev in (0, 1):
        qin_ref, kin_ref, kst_ref, dec_ref = dir_refs[rev]
        for blk in range(nblk):
            rows = slice(blk * SCAN_BLOCK, (blk + 1) * SCAN_BLOCK)
            bb = b[rev, blk]
            if latent:
                qin_ref[rows, :] = (q[rows] * jnp.exp(bb)).astype(BF16)
                kin_ref[rows, :] = (k[rows] * jnp.exp(-bb)).astype(BF16)
            ends = []
            for c in range(ncb):
                lo = c * GLA_CHUNK
                b_c = bb[lo:lo + GLA_CHUNK]
                b_end = b_c[0:1] if rev else b_c[GLA_CHUNK - 1:GLA_CHUNK]
                kst = k[blk * SCAN_BLOCK + lo:blk * SCAN_BLOCK + lo + GLA_CHUNK] * jnp.exp(b_end - b_c)
                kst_ref[blk * ncb + c] = kst.T.astype(BF16)
                ends.append(b_end)
            dec_ref[blk] = jnp.exp(jnp.concatenate(ends, axis=0))


def _gla_proj(x2d, mods, w, wdec, bdec, mod_row, latent, tm, dk, dv, q_scale, cast=()):
    n, d = x2d.shape
    steps = n // tm
    ncb = SCAN_BLOCK // GLA_CHUNK
    row = lambda i: (i, 0)
    row3 = lambda i: (i, 0, 0)
    slabs = [t.reshape(t.shape[0], steps, t.shape[1] // steps, t.shape[2]) for t in cast]
    slab_in = [pl.BlockSpec((t.shape[0], 1) + t.shape[2:], lambda i: (0, i, 0, 0)) for t in slabs]
    slab_out = [(jax.ShapeDtypeStruct(t.shape[1:], BF16), pl.BlockSpec((1,) + t.shape[2:], row3))
                for t in slabs for _ in range(t.shape[0])]
    tok = lambda c: (jax.ShapeDtypeStruct((n, c), BF16), pl.BlockSpec((tm, c), row))
    kst = (jax.ShapeDtypeStruct((n // GLA_CHUNK, dk, GLA_CHUNK), BF16),
           pl.BlockSpec((tm // GLA_CHUNK, dk, GLA_CHUNK), row3))
    dec = (jax.ShapeDtypeStruct((n // SCAN_BLOCK, ncb, dk), F32),
           pl.BlockSpec((tm // SCAN_BLOCK, ncb, dk), row3))
    per_dir = [tok(dk), tok(dk), kst, dec] if latent else [kst, dec]
    outs = ([tok(dv), tok(dv)] if latent else [tok(dv)]) + per_dir + per_dir
    res = pl.pallas_call(
        functools.partial(_gla_proj_kernel, latent=latent, dk=dk, dv=dv, q_scale=q_scale,
                          cast_layers=tuple(t.shape[0] for t in cast)),
        grid=(steps,),
        in_specs=[
            pl.BlockSpec((tm, d), row),
            pl.BlockSpec((1,) + mods.shape[1:], lambda i: (mod_row(i), 0, 0)),
            _const_spec(w.shape), _const_spec(wdec.shape), _const_spec(bdec.shape),
        ] + slab_in,
        out_specs=[o[1] for o in outs] + [o[1] for o in slab_out],
        out_shape=[o[0] for o in outs] + [o[0] for o in slab_out],
        compiler_params=_params("parallel"),
        name="gla_in_proj" if latent else "gla_ctx_proj",
    )(x2d, mods, w, wdec, bdec, *slabs)
    n_out = len(outs)
    shapes = [t.shape[1:] for t in cast for _ in range(t.shape[0])]
    return res[:n_out], [r.reshape(s) for r, s in zip(res[n_out:], shapes)]


def _decay_columns(dec_row, dvh):
    dkh = dec_row.shape[1]
    dcol = jnp.broadcast_to(dec_row, (dkh, dkh)).T
    return jnp.concatenate([dcol] * (dvh // dkh), axis=1)


def _gla_scan_kernel(*refs, reverse, add_prev):
    qin_ref, kin_ref, kst_ref, v_ref, dec_ref, kstc_ref, vc_ref, decc_ref = refs[:8]
    if add_prev:
        prev_ref, o_ref, s_ref = refs[8:]
    else:
        o_ref, s_ref = refs[8:]
    dkh, dvh = s_ref.shape[1:]
    heads = range(GLA_HEADS)
    ks = [slice(h * dkh, (h + 1) * dkh) for h in heads]
    vs = [slice(h * dvh, (h + 1) * dvh) for h in heads]

    def order(nc):
        return list(range(nc - 1, -1, -1) if reverse else range(nc))

    def rows(c):
        return slice(c * GLA_CHUNK, (c + 1) * GLA_CHUNK)

    @pl.when(pl.program_id(1) == 0)
    def _():
        chunks = order(vc_ref.shape[1] // GLA_CHUNK)
        upd = {(c, h): _dot(kstc_ref[c, ks[h], :], vc_ref[0, rows(c), vs[h]]) for c in chunks for h in heads}
        for h in heads:
            s = jnp.zeros((dkh, dvh), F32)
            for c in chunks:
                s = s * _decay_columns(decc_ref[0, c:c + 1, ks[h]], dvh) + upd[c, h]
            s_ref[h] = s

    chunks = order(v_ref.shape[1] // GLA_CHUNK)
    pairs = [(c, h) for c in chunks for h in heads]
    r64 = lax.broadcasted_iota(jnp.int32, (GLA_CHUNK, GLA_CHUNK), 0)
    c64 = lax.broadcasted_iota(jnp.int32, (GLA_CHUNK, GLA_CHUNK), 1)
    mask = (c64 >= r64) if reverse else (c64 <= r64)
    att = {(c, h): lax.dot_general(qin_ref[0, rows(c), ks[h]], kin_ref[0, rows(c), ks[h]],
                                   (((1,), (1,)), ((), ())), preferred_element_type=F32) for c, h in pairs}
    upd = {(c, h): _dot(kst_ref[c, ks[h], :], v_ref[0, rows(c), vs[h]]) for c, h in pairs}
    dcol = {(c, h): _decay_columns(dec_ref[0, c:c + 1, ks[h]], dvh) for c, h in pairs}
    att = {p: jnp.where(mask, att[p], 0.0).astype(BF16) for p in pairs}
    intra = {(c, h): _dot(att[c, h], v_ref[0, rows(c), vs[h]]) for c, h in pairs}
    state = [s_ref[h] for h in heads]
    for c in chunks:
        outs = []
        for h in heads:
            outs.append(intra[c, h] + _dot(qin_ref[0, rows(c), ks[h]], state[h].astype(BF16)))
            state[h] = state[h] * dcol[c, h] + upd[c, h]
        o = jnp.concatenate(outs, axis=1)
        if add_prev:
            o = o + prev_ref[0, rows(c), :].astype(F32)
        o_ref[0, rows(c), :] = o.astype(o_ref.dtype)
    for h in heads:
        s_ref[h] = state[h]


def _gla_scan(qin, kin, kst, v, dec, kstc, vc, decc, reverse, prev=None):
    bsz, l, dk = qin.shape
    dv = v.shape[2]
    lc = vc.shape[1]
    tb = SCAN_BLOCK
    nblk = l // tb
    ncb = tb // GLA_CHUNK

    def pos(j):
        return (nblk - 1 - j) if reverse else j

    blk = lambda b, j: (b, pos(j), 0)
    flat = lambda b, j: (b * nblk + pos(j), 0, 0)
    ctx = lambda b, j: (b, 0, 0)
    in_specs = [
        pl.BlockSpec((1, tb, dk), blk), pl.BlockSpec((1, tb, dk), blk),
        pl.BlockSpec((ncb, dk, GLA_CHUNK), flat), pl.BlockSpec((1, tb, dv), blk),
        pl.BlockSpec((1, ncb, dk), flat),
        pl.BlockSpec((lc // GLA_CHUNK, dk, GLA_CHUNK), ctx), pl.BlockSpec((1, lc, dv), ctx),
        pl.BlockSpec((1, lc // GLA_CHUNK, dk), ctx),
    ]
    args = [qin, kin, kst, v, dec, kstc, vc, decc]
    if prev is not None:
        in_specs.append(pl.BlockSpec((1, tb, dv), blk))
        args.append(prev)
    return pl.pallas_call(
        functools.partial(_gla_scan_kernel, reverse=reverse, add_prev=prev is not None),
        grid=(bsz, nblk),
        in_specs=in_specs,
        out_specs=pl.BlockSpec((1, tb, dv), blk),
        out_shape=jax.ShapeDtypeStruct((bsz, l, dv), BF16),
        scratch_shapes=[pltpu.VMEM((GLA_HEADS, dk // GLA_HEADS, dv // GLA_HEADS), F32)],
        compiler_params=_params("arbitrary", "arbitrary"),
        name="gla_scan_bwd" if reverse else "gla_scan_fwd",
    )(*args)


def _ffn_kernel(*refs, gla_prologue):
    if gla_prologue:
        (o_ref, gate_ref, x_ref, mod_ref, ng_ref, wout_ref, lng0_ref, lnb0_ref,
         wg_ref, wu_ref, wd_ref, lng_ref, lnb_ref, out_ref, hid_ref) = refs
    else:
        h_ref, mod_ref, wg_ref, wu_ref, wd_ref, lng_ref, lnb_ref, out_ref, hid_ref = refs
    m = mod_ref[0]
    hidden = wg_ref.shape[1]
    nsub = out_ref.shape[0] // FFN_SUB

    def head(lo, size):
        rows = slice(lo, lo + size)
        if gla_prologue:
            o = o_ref[rows, :].astype(F32)
            dvh = o.shape[1] // GLA_HEADS
            normed = []
            for hd in range(GLA_HEADS):
                o_h = o[:, hd * dvh:(hd + 1) * dvh]
                ms = jnp.mean(o_h * o_h, axis=-1, keepdims=True)
                normed.append(o_h * lax.rsqrt(ms + 1e-6) * ng_ref[...])
            gated = (jnp.concatenate(normed, axis=1) * gate_ref[rows, :].astype(F32)).astype(BF16)
            y = _dot(gated, wout_ref[...])
            h = _layer_norm(ALPHA * x_ref[rows, :] + m[2:3] * y, lng0_ref[...], lnb0_ref[...])
        else:
            h = h_ref[rows, :]
        af = h * (1.0 + m[4:5]) + m[3:4]
        return h, af.astype(BF16), _lane_fold(af)

    def tail(lo, h, f, after=None):
        res = ALPHA * h + m[5:6] * f
        if after is not None:
            res = _order_after(res, after)
        out_ref[lo:lo + h.shape[0], :] = _layer_norm(res, lng_ref[...], lnb_ref[...])

    head_rows = FFN_SUB // len(FFN_HEAD_AT)
    tail_rows = FFN_SUB // FFN_TAIL_PIECES
    h, a, _ = head(0, FFN_SUB)
    pending = None
    for sub in range(nsub):
        nxt, done_tail = [], 0
        for j, lo in enumerate(range(0, hidden, FFN_COLS)):
            piece = None
            if sub + 1 < nsub and j in FFN_HEAD_AT:
                piece = head((sub + 1) * FFN_SUB + FFN_HEAD_AT.index(j) * head_rows, head_rows)
                nxt.append(piece)
            g = _dot(a, wg_ref[:, lo:lo + FFN_COLS])
            u = _dot(a, wu_ref[:, lo:lo + FFN_COLS])
            prod = _silu(g) * u
            if piece is not None:
                prod = jnp.concatenate([_order_after(prod[0:head_rows], piece[2]), prod[head_rows:]], axis=0)
            hid_ref[sub, :, lo:lo + FFN_COLS] = prod.astype(BF16)
            if pending is not None and piece is None and done_tail < FFN_TAIL_PIECES:
                p_lo, p_h, p_f = pending
                r = slice(done_tail * tail_rows, (done_tail + 1) * tail_rows)
                tail(p_lo + done_tail * tail_rows, p_h[r], p_f[r], after=g)
                done_tail += 1
        assert pending is None or done_tail == FFN_TAIL_PIECES
        pending = (sub * FFN_SUB, h, _dot(hid_ref[sub], wd_ref[...]))
        if nxt:
            h = jnp.concatenate([p[0] for p in nxt], axis=0)
            a = jnp.concatenate([p[1] for p in nxt], axis=0)
    tail(*pending)


def _ffn(acts, mods, consts, tiles_per_batch, gla_prologue):
    n, d = acts[0].shape
    tm = FFN_TILE
    hidden = consts[-5].shape[1]
    row = lambda i: (i, 0)
    mod_spec = pl.BlockSpec((1,) + mods.shape[1:], lambda i: (i // tiles_per_batch, 0, 0))
    in_specs = ([pl.BlockSpec((tm, t.shape[1]), row) for t in acts] + [mod_spec]
                + [_const_spec(t.shape) for t in consts])
    return pl.pallas_call(
        functools.partial(_ffn_kernel, gla_prologue=gla_prologue),
        grid=(n // tm,),
        in_specs=in_specs,
        out_specs=pl.BlockSpec((tm, d), row),
        out_shape=jax.ShapeDtypeStruct((n, d), F32),
        scratch_shapes=[pltpu.VMEM((tm // FFN_SUB, FFN_SUB, hidden), BF16)],
        compiler_params=_params("parallel"),
        name="gla_out_ffn" if gla_prologue else "swiglu_ffn",
    )(*acts, mods, *consts)


def _gmlp_kernel(h_ref, mod_ref, win_ref, vg_ref, vb_ref, ws_ref, bs_ref, wout_ref, lng_ref, lnb_ref,
                 o_ref, v_scr):
    m = mod_ref[0]
    h = h_ref[...]
    tm = h.shape[0]
    width = v_scr.shape[1]
    gw = width // GM_GROUPS
    a = (h * (1.0 + m[1:2]) + m[0:1]).astype(BF16)

    def gelu(z):
        return 0.5 * z * (1.0 + lax.erf(z * (0.5 ** 0.5)))

    part = None
    for lo in range(0, width, GMLP_COLS):
        z = gelu(_dot(a, win_ref[:, width + lo:width + lo + GMLP_COLS]))
        v_scr[:, lo:lo + GMLP_COLS] = z
        part = _lane_fold(z) if part is None else part + _lane_fold(z)
    mu = jnp.sum(part, axis=-1, keepdims=True) * (1.0 / width)
    u_next = _dot(a, win_ref[:, 0:gw])
    part = None
    for lo in range(0, width, GMLP_COLS):
        vc = v_scr[:, lo:lo + GMLP_COLS] - mu
        part = _lane_fold(vc * vc) if part is None else part + _lane_fold(vc * vc)
    rstd = lax.rsqrt(jnp.sum(part, axis=-1, keepdims=True) * (1.0 / width) + 1e-5)
    y = None
    for gi in range(GM_GROUPS):
        cols = slice(gi * gw, (gi + 1) * gw)
        u = gelu(u_next)
        if gi + 1 < GM_GROUPS:
            u_next = _dot(a, win_ref[:, (gi + 1) * gw:(gi + 2) * gw])
        bias = jnp.concatenate([bs_ref[gi]] * (gw // LANES), axis=1)
        gated = []
        for t in range(tm // GM_CHUNK):
            rows = slice(t * GM_CHUNK, (t + 1) * GM_CHUNK)
            vn = (v_scr[rows, cols] - mu[rows]) * rstd[rows] * vg_ref[:, cols] + vb_ref[:, cols]
            s = _dot(ws_ref[gi], vn.astype(BF16)) + bias
            gated.append((u[rows] * s).astype(BF16))
        y_g = _dot(jnp.concatenate(gated, axis=0), wout_ref[cols, :])
        y = y_g if y is None else y + y_g
    res = ALPHA * h + m[2:3] * y
    o_ref[...] = _layer_norm(res, lng_ref[...], lnb_ref[...])


def _gmlp(h2d, mods, consts, tiles_per_batch):
    n, d = h2d.shape
    tm = GMLP_TILE
    width = consts[5].shape[0]
    row = lambda i: (i, 0)
    return pl.pallas_call(
        _gmlp_kernel,
        grid=(n // tm,),
        in_specs=[
            pl.BlockSpec((tm, d), row),
            pl.BlockSpec((1,) + mods.shape[1:], lambda i: (i // tiles_per_batch, 0, 0)),
        ] + [_const_spec(t.shape) for t in consts],
        out_specs=pl.BlockSpec((tm, d), row),
        out_shape=jax.ShapeDtypeStruct((n, d), F32),
        scratch_shapes=[pltpu.VMEM((tm, width), F32)],
        compiler_params=_params("parallel"),
        name="gmlp_mixer",
    )(h2d, mods, *consts)


def kernel(x, c, ctx, c_ctx, mod_w, mod_b, ln_g, ln_b, gla_w_in, gla_w_decay, gla_b_decay, gla_norm_g,
           gla_w_out, gm_w_in, gm_ln_g, gm_ln_b, gm_w_s, gm_b_s, gm_w_out, ffn_w_gate, ffn_w_up, ffn_w_down):
    bsz, l, d = x.shape
    lc = ctx.shape[1]
    n = bsz * l
    assert bsz + 1 <= COND_ROWS
    dk = gla_w_decay.shape[-1]
    dv = gla_w_out.shape[1]
    q_scale = (dk // GLA_HEADS) ** -0.5
    vec = lambda t: t.reshape(1, -1)

    cond = jnp.concatenate([c, c_ctx[None], jnp.zeros((COND_ROWS - bsz - 1, d), F32)], axis=0)
    mods = _adaln(cond, mod_w, mod_b).reshape(DEPTH, COND_ROWS, 6, d)

    w_in = gla_w_in[0]
    o_a = dk + dv
    o_q = o_a + 2 * GLA_RANK
    w_k, w_v, w_a = w_in[:, :dk], w_in[:, dk:o_a], w_in[:, o_a:o_q]
    w_q, w_r = w_in[:, o_q:o_q + dk], w_in[:, o_q + dk:]
    w_a = jnp.pad(w_a, ((0, 0), (0, LANES - 2 * GLA_RANK)))
    w_lat = jnp.concatenate([w_k, w_q, w_a, w_v, w_r], axis=1).astype(BF16)
    w_ctx = jnp.concatenate([w_k, w_a, w_v], axis=1).astype(BF16)
    wdec = jnp.zeros((2, LANES, dk), F32)
    wdec = wdec.at[0, :GLA_RANK].set(gla_w_decay[0, 0]).at[1, GLA_RANK:2 * GLA_RANK].set(gla_w_decay[0, 1])
    wdec = wdec.astype(BF16)
    bdec = gla_b_decay[0].reshape(2, 1, dk)

    tpb = l // PROJ_TILE
    later = (gla_w_out, gm_w_in, gm_w_out, ffn_w_gate, ffn_w_up, ffn_w_down)
    ((v, gate, qin_f, kin_f, kst_f, dec_f, qin_b, kin_b, kst_b, dec_b),
     (w_out, gm_in, gm_out, wg0, wg1, wu0, wu1, wd0, wd1)) = _gla_proj(
        x.reshape(n, d), mods[0], w_lat, wdec, bdec, lambda i: i // tpb, True, PROJ_TILE, dk, dv, q_scale,
        cast=later)
    (vc, kstc_f, decc_f, kstc_b, decc_b), _ = _gla_proj(
        ctx.reshape(bsz * lc, d), mods[0], w_ctx, wdec, bdec, lambda i: bsz, False, lc, dk, dv, q_scale)
    b3 = lambda t: t.reshape(bsz, l, -1)
    v3, vc3 = b3(v), vc.reshape(bsz, lc, dv)
    o_b = _gla_scan(b3(qin_b), b3(kin_b), kst_b, v3, dec_b, kstc_b, vc3, decc_b, True)
    o = _gla_scan(b3(qin_f), b3(kin_f), kst_f, v3, dec_f, kstc_f, vc3, decc_f, False, prev=o_b)

    ffn_w = ((wg0, wu0, wd0), (wg1, wu1, wd1))
    ffn_consts = lambda i: ffn_w[i] + (vec(ln_g[i, 1]), vec(ln_b[i, 1]))
    gla_consts = (vec(gla_norm_g[0]), w_out, vec(ln_g[0, 0]), vec(ln_b[0, 0]))
    tpf = l // FFN_TILE
    h = _ffn((o.reshape(n, dv), gate, x.reshape(n, d)), mods[0], gla_consts + ffn_consts(0), tpf, True)

    bs = jnp.broadcast_to(gm_b_s[0].T[:, :, None], (GM_GROUPS, GM_CHUNK, LANES))
    gm_consts = (gm_in, vec(gm_ln_g[0]), vec(gm_ln_b[0]), gm_w_s[0].astype(BF16), bs,
                 gm_out, vec(ln_g[1, 0]), vec(ln_b[1, 0]))
    h = _gmlp(h, mods[1], gm_consts, l // GMLP_TILE)
    h = _ffn((h,), mods[1], ffn_consts(1), tpf, False)
    return h.reshape(bsz, l, d)
```

```python
import functools

import jax
import jax.numpy as jnp
from jax import lax
from jax.experimental import pallas as pl
from jax.experimental.pallas import tpu as pltpu

F32 = jnp.float32
BF16 = jnp.bfloat16

DEPTH = 2
ALPHA = (2 * DEPTH) ** 0.25
GLA_HEADS = 4
GLA_RANK = 16
GLA_GATE_NORM = 16.0
GLA_CHUNK = 64
GM_GROUPS = 4
GM_CHUNK = 128
COND_ROWS = 8
LANES = 128
VMEM_LIMIT = 56 * 1024 * 1024

PROJ_TILE = 512
SCAN_BLOCK = 512
DECAY_BLOCK = 256
FFN_TILE = 1024
FFN_SUB = 512
FFN_HEAD_AT = (3, 6)
FFN_TAIL_PIECES = 8
FFN_COLS = 256
GMLP_TILE = 512
GMLP_COLS = 512


def _dot(a, b):
    return jnp.dot(a, b, preferred_element_type=F32)


def _layer_norm(x, g, b, eps=1e-5):
    mu = jnp.mean(x, axis=-1, keepdims=True)
    xc = x - mu
    var = jnp.mean(xc * xc, axis=-1, keepdims=True)
    return xc * lax.rsqrt(var + eps) * g + b


def _silu(x):
    return x * jax.nn.sigmoid(x)


def _lane_fold(t):
    acc = t[:, 0:LANES]
    for j in range(1, t.shape[1] // LANES):
        acc = acc + t[:, j * LANES:(j + 1) * LANES]
    return acc


def _order_after(x, dep):
    bits = pltpu.bitcast(dep[0:x.shape[0], 0:LANES], jnp.uint32)
    half = jnp.uint32(16)
    zero = pltpu.bitcast(lax.shift_right_logical(lax.shift_right_logical(bits, half), half), F32)
    return jnp.concatenate([x[:, 0:LANES] + zero, x[:, LANES:]], axis=1)


def _params(*sem):
    return pltpu.CompilerParams(dimension_semantics=sem, vmem_limit_bytes=VMEM_LIMIT)


def _const_spec(shape):
    nd = len(shape)
    return pl.BlockSpec(shape, lambda *_: (0,) * nd, pipeline_mode=pl.Buffered(1))


def _adaln_kernel(cond_ref, w_ref, b_ref, o_ref):
    s = _silu(cond_ref[...]).astype(BF16)
    o_ref[0] = _dot(s, w_ref[0].astype(BF16)) + b_ref[0]


def _adaln(cond, mod_w, mod_b):
    depth, d, n = mod_w.shape
    tn = n // 4
    return pl.pallas_call(
        _adaln_kernel,
        grid=(depth, n // tn),
        in_specs=[
            pl.BlockSpec((COND_ROWS, d), lambda i, j: (0, 0)),
            pl.BlockSpec((1, d, tn), lambda i, j: (i, 0, j)),
            pl.BlockSpec((1, 1, tn), lambda i, j: (i, 0, j)),
        ],
        out_specs=pl.BlockSpec((1, COND_ROWS, tn), lambda i, j: (i, 0, j)),
        out_shape=jax.ShapeDtypeStruct((depth, COND_ROWS, n), F32),
        compiler_params=_params("arbitrary", "arbitrary"),
        name="adaln",
    )(cond, mod_w, mod_b.reshape(depth, 1, n))


def _log_sigmoid(z):
    return jnp.minimum(z, 0.0) - jnp.log(1.0 + jnp.exp(-jnp.abs(z)))


def _chunk_tri(n, reverse):
    row = lax.broadcasted_iota(jnp.int32, (n, n), 0)
    col = lax.broadcasted_iota(jnp.int32, (n, n), 1)
    shift = GLA_CHUNK.bit_length() - 1
    same = jnp.right_shift(row, shift) == jnp.right_shift(col, shift)
    return jnp.where(same & ((col >= row) if reverse else (col <= row)), 1.0, 0.0).astype(BF16)


def _gla_proj_kernel(x_ref, mod_ref, w_ref, wdec_ref, bdec_ref, *refs, latent, dk, dv, q_scale, cast_layers):
    n_src, n_dst = len(cast_layers), sum(cast_layers)
    dsts = iter(refs[len(refs) - n_dst:])
    for src, layers in zip(refs[:n_src], cast_layers):
        for layer in range(layers):
            next(dsts)[0] = src[layer, 0].astype(BF16)
    out_refs = refs[n_src:len(refs) - n_dst]
    m = mod_ref[0]
    a = (x_ref[...] * (1.0 + m[1:2]) + m[0:1]).astype(BF16)
    tm = a.shape[0]
    nblk = tm // DECAY_BLOCK
    ncb = DECAY_BLOCK // GLA_CHUNK
    o_a = 2 * dk if latent else dk
    o_v = o_a + LANES
    if latent:
        v_ref, gate_ref = out_refs[:2]
        dir_refs = (out_refs[2:6], out_refs[6:10])
    else:
        v_ref = out_refs[0]
        dir_refs = ((None, None) + tuple(out_refs[1:3]), (None, None) + tuple(out_refs[3:5]))
    k = _dot(a, w_ref[:, 0:dk])
    if latent:
        q = _dot(a, w_ref[:, dk:2 * dk]) * q_scale
    a_lr = _dot(a, w_ref[:, o_a:o_a + LANES]).astype(BF16)
    z = [_dot(a_lr, wdec_ref[rev]) + bdec_ref[rev] for rev in (0, 1)]
    v_ref[...] = _dot(a, w_ref[:, o_v:o_v + dv]).astype(BF16)
    g_parts = []
    for rev in (0, 1):
        g = _log_sigmoid(z[rev]) * (1.0 / GLA_GATE_NORM)
        g_hi = g.astype(BF16)
        g_parts.append((g_hi, (g - g_hi.astype(F32)).astype(BF16)))
    tris = (_chunk_tri(DECAY_BLOCK, False), _chunk_tri(DECAY_BLOCK, True))
    b = {}
    for rev in (0, 1):
        for blk in range(nblk):
            rows = slice(blk * DECAY_BLOCK, (blk + 1) * DECAY_BLOCK)
            b[rev, blk] = _dot(tris[rev], g_parts[rev][0][rows]) + _dot(tris[rev], g_parts[rev][1][rows])
    if latent:
        gate_ref[...] = _silu(_dot(a, w_ref[:, o_v + dv:o_v + 2 * dv])).astype(BF16)
    for rev in (0, 1):
        qin_ref, kin_ref, kst_ref, dec_ref = dir_refs[rev]
        group_chunks = dec_ref.shape[1]
        for blk in range(nblk):
            rows = slice(blk * DECAY_BLOCK, (blk + 1) * DECAY_BLOCK)
            bb = b[rev, blk]
            if latent:
                qin_ref[rows, :] = (q[rows] * jnp.exp(bb)).astype(BF16)
                kin_ref[rows, :] = (k[rows] * jnp.exp(-bb)).astype(BF16)
            ends = []
            for c in range(ncb):
                lo = c * GLA_CHUNK
                b_c = bb[lo:lo + GLA_CHUNK]
                b_end = b_c[0:1] if rev else b_c[GLA_CHUNK - 1:GLA_CHUNK]
                kst = k[blk * DECAY_BLOCK + lo:blk * DECAY_BLOCK + lo + GLA_CHUNK] * jnp.exp(b_end - b_c)
                kst_ref[blk * ncb + c] = kst.T.astype(BF16)
                ends.append(b_end)
            first = blk * ncb
            dec_ref[first // group_chunks, first % group_chunks:first % group_chunks + ncb, :] = jnp.exp(
                jnp.concatenate(ends, axis=0))


def _gla_proj(x2d, mods, w, wdec, bdec, mod_row, latent, tm, dk, dv, q_scale, cast=()):
    n, d = x2d.shape
    steps = n // tm
    group = min(SCAN_BLOCK, tm)
    row = lambda i: (i, 0)
    row3 = lambda i: (i, 0, 0)
    slabs = [t.reshape(t.shape[0], steps, t.shape[1] // steps, t.shape[2]) for t in cast]
    slab_in = [pl.BlockSpec((t.shape[0], 1) + t.shape[2:], lambda i: (0, i, 0, 0)) for t in slabs]
    slab_out = [(jax.ShapeDtypeStruct(t.shape[1:], BF16), pl.BlockSpec((1,) + t.shape[2:], row3))
                for t in slabs for _ in range(t.shape[0])]
    tok = lambda c: (jax.ShapeDtypeStruct((n, c), BF16), pl.BlockSpec((tm, c), row))
    kst = (jax.ShapeDtypeStruct((n // GLA_CHUNK, dk, GLA_CHUNK), BF16),
           pl.BlockSpec((tm // GLA_CHUNK, dk, GLA_CHUNK), row3))
    dec = (jax.ShapeDtypeStruct((n // group, group // GLA_CHUNK, dk), F32),
           pl.BlockSpec((tm // group, group // GLA_CHUNK, dk), row3))
    per_dir = [tok(dk), tok(dk), kst, dec] if latent else [kst, dec]
    outs = ([tok(dv), tok(dv)] if latent else [tok(dv)]) + per_dir + per_dir
    res = pl.pallas_call(
        functools.partial(_gla_proj_kernel, latent=latent, dk=dk, dv=dv, q_scale=q_scale,
                          cast_layers=tuple(t.shape[0] for t in cast)),
        grid=(steps,),
        in_specs=[
            pl.BlockSpec((tm, d), row),
            pl.BlockSpec((1,) + mods.shape[1:], lambda i: (mod_row(i), 0, 0)),
            _const_spec(w.shape), _const_spec(wdec.shape), _const_spec(bdec.shape),
        ] + slab_in,
        out_specs=[o[1] for o in outs] + [o[1] for o in slab_out],
        out_shape=[o[0] for o in outs] + [o[0] for o in slab_out],
        compiler_params=_params("parallel"),
        name="gla_in_proj" if latent else "gla_ctx_proj",
    )(x2d, mods, w, wdec, bdec, *slabs)
    n_out = len(outs)
    shapes = [t.shape[1:] for t in cast for _ in range(t.shape[0])]
    return res[:n_out], [r.reshape(s) for r, s in zip(res[n_out:], shapes)]


def _decay_columns(dec_row, dvh):
    dkh = dec_row.shape[1]
    dcol = jnp.broadcast_to(dec_row, (dkh, dkh)).T
    return jnp.concatenate([dcol] * (dvh // dkh), axis=1)


def _gla_scan_kernel(*refs, reverse, add_prev):
    qin_ref, kin_ref, kst_ref, v_ref, dec_ref, kstc_ref, vc_ref, decc_ref = refs[:8]
    if add_prev:
        prev_ref, o_ref, s_ref = refs[8:]
    else:
        o_ref, s_ref = refs[8:]
    dkh, dvh = s_ref.shape[1:]
    heads = range(GLA_HEADS)
    ks = [slice(h * dkh, (h + 1) * dkh) for h in heads]
    vs = [slice(h * dvh, (h + 1) * dvh) for h in heads]

    def order(nc):
        return list(range(nc - 1, -1, -1) if reverse else range(nc))

    def rows(c):
        return slice(c * GLA_CHUNK, (c + 1) * GLA_CHUNK)

    @pl.when(pl.program_id(1) == 0)
    def _():
        chunks = order(vc_ref.shape[1] // GLA_CHUNK)
        upd = {(c, h): _dot(kstc_ref[c, ks[h], :], vc_ref[0, rows(c), vs[h]]) for c in chunks for h in heads}
        for h in heads:
            s = jnp.zeros((dkh, dvh), F32)
            for c in chunks:
                s = s * _decay_columns(decc_ref[0, c:c + 1, ks[h]], dvh) + upd[c, h]
            s_ref[h] = s

    chunks = order(v_ref.shape[1] // GLA_CHUNK)
    pairs = [(c, h) for c in chunks for h in heads]
    r64 = lax.broadcasted_iota(jnp.int32, (GLA_CHUNK, GLA_CHUNK), 0)
    c64 = lax.broadcasted_iota(jnp.int32, (GLA_CHUNK, GLA_CHUNK), 1)
    mask = (c64 >= r64) if reverse else (c64 <= r64)
    att = {(c, h): lax.dot_general(qin_ref[0, rows(c), ks[h]], kin_ref[0, rows(c), ks[h]],
                                   (((1,), (1,)), ((), ())), preferred_element_type=F32) for c, h in pairs}
    upd = {(c, h): _dot(kst_ref[c, ks[h], :], v_ref[0, rows(c), vs[h]]) for c, h in pairs}
    dcol = {(c, h): _decay_columns(dec_ref[0, c:c + 1, ks[h]], dvh) for c, h in pairs}
    att = {p: jnp.where(mask, att[p], 0.0).astype(BF16) for p in pairs}
    intra = {(c, h): _dot(att[c, h], v_ref[0, rows(c), vs[h]]) for c, h in pairs}
    state = [s_ref[h] for h in heads]
    for c in chunks:
        outs = []
        for h in heads:
            outs.append(intra[c, h] + _dot(qin_ref[0, rows(c), ks[h]], state[h].astype(BF16)))
            state[h] = state[h] * dcol[c, h] + upd[c, h]
        o = jnp.concatenate(outs, axis=1)
        if add_prev:
            o = o + prev_ref[0, rows(c), :].astype(F32)
        o_ref[0, rows(c), :] = o.astype(o_ref.dtype)
    for h in heads:
        s_ref[h] = state[h]


def _gla_scan(qin, kin, kst, v, dec, kstc, vc, decc, reverse, prev=None):
    bsz, l, dk = qin.shape
    dv = v.shape[2]
    lc = vc.shape[1]
    tb = SCAN_BLOCK
    nblk = l // tb
    ncb = tb // GLA_CHUNK

    def pos(j):
        return (nblk - 1 - j) if reverse else j

    blk = lambda b, j: (b, pos(j), 0)
    flat = lambda b, j: (b * nblk + pos(j), 0, 0)
    ctx = lambda b, j: (b, 0, 0)
    in_specs = [
        pl.BlockSpec((1, tb, dk), blk), pl.BlockSpec((1, tb, dk), blk),
        pl.BlockSpec((ncb, dk, GLA_CHUNK), flat), pl.BlockSpec((1, tb, dv), blk),
        pl.BlockSpec((1, ncb, dk), flat),
        pl.BlockSpec((lc // GLA_CHUNK, dk, GLA_CHUNK), ctx), pl.BlockSpec((1, lc, dv), ctx),
        pl.BlockSpec((1, lc // GLA_CHUNK, dk), ctx),
    ]
    args = [qin, kin, kst, v, dec, kstc, vc, decc]
    if prev is not None:
        in_specs.append(pl.BlockSpec((1, tb, dv), blk))
        args.append(prev)
    return pl.pallas_call(
        functools.partial(_gla_scan_kernel, reverse=reverse, add_prev=prev is not None),
        grid=(bsz, nblk),
        in_specs=in_specs,
        out_specs=pl.BlockSpec((1, tb, dv), blk),
        out_shape=jax.ShapeDtypeStruct((bsz, l, dv), BF16),
        scratch_shapes=[pltpu.VMEM((GLA_HEADS, dk // GLA_HEADS, dv // GLA_HEADS), F32)],
        compiler_params=_params("arbitrary", "arbitrary"),
        name="gla_scan_bwd" if reverse else "gla_scan_fwd",
    )(*args)


def _ffn_kernel(*refs, gla_prologue):
    if gla_prologue:
        (o_ref, gate_ref, x_ref, mod_ref, ng_ref, wout_ref, lng0_ref, lnb0_ref,
         wg_ref, wu_ref, wd_ref, lng_ref, lnb_ref, out_ref, hid_ref) = refs
    else:
        h_ref, mod_ref, wg_ref, wu_ref, wd_ref, lng_ref, lnb_ref, out_ref, hid_ref = refs
    m = mod_ref[0]
    hidden = wg_ref.shape[1]
    nsub = out_ref.shape[0] // FFN_SUB

    def head(lo, size):
        rows = slice(lo, lo + size)
        if gla_prologue:
            o = o_ref[rows, :].astype(F32)
            dvh = o.shape[1] // GLA_HEADS
            normed = []
            for hd in range(GLA_HEADS):
                o_h = o[:, hd * dvh:(hd + 1) * dvh]
                ms = jnp.mean(o_h * o_h, axis=-1, keepdims=True)
                normed.append(o_h * lax.rsqrt(ms + 1e-6) * ng_ref[...])
            gated = (jnp.concatenate(normed, axis=1) * gate_ref[rows, :].astype(F32)).astype(BF16)
            y = _dot(gated, wout_ref[...])
            h = _layer_norm(ALPHA * x_ref[rows, :] + m[2:3] * y, lng0_ref[...], lnb0_ref[...])
        else:
            h = h_ref[rows, :]
        af = h * (1.0 + m[4:5]) + m[3:4]
        return h, af.astype(BF16), _lane_fold(af)

    def tail(lo, h, f, after=None):
        res = ALPHA * h + m[5:6] * f
        if after is not None:
            res = _order_after(res, after)
        out_ref[lo:lo + h.shape[0], :] = _layer_norm(res, lng_ref[...], lnb_ref[...])

    head_rows = FFN_SUB // len(FFN_HEAD_AT)
    tail_rows = FFN_SUB // FFN_TAIL_PIECES
    h, a, _ = head(0, FFN_SUB)
    pending = None
    for sub in range(nsub):
        nxt, done_tail = [], 0
        for j, lo in enumerate(range(0, hidden, FFN_COLS)):
            piece = None
            if sub + 1 < nsub and j in FFN_HEAD_AT:
                piece = head((sub + 1) * FFN_SUB + FFN_HEAD_AT.index(j) * head_rows, head_rows)
                nxt.append(piece)
            g = _dot(a, wg_ref[:, lo:lo + FFN_COLS])
            u = _dot(a, wu_ref[:, lo:lo + FFN_COLS])
            prod = _silu(g) * u
            if piece is not None:
                prod = jnp.concatenate([_order_after(prod[0:head_rows], piece[2]), prod[head_rows:]], axis=0)
            hid_ref[sub, :, lo:lo + FFN_COLS] = prod.astype(BF16)
            if pending is not None and piece is None and done_tail < FFN_TAIL_PIECES:
                p_lo, p_h, p_f = pending
                r = slice(done_tail * tail_rows, (done_tail + 1) * tail_rows)
                tail(p_lo + done_tail * tail_rows, p_h[r], p_f[r], after=g)
                done_tail += 1
        assert pending is None or done_tail == FFN_TAIL_PIECES
        pending = (sub * FFN_SUB, h, _dot(hid_ref[sub], wd_ref[...]))
        if nxt:
            h = jnp.concatenate([p[0] for p in nxt], axis=0)
            a = jnp.concatenate([p[1] for p in nxt], axis=0)
    tail(*pending)


def _ffn(acts, mods, consts, tiles_per_batch, gla_prologue):
    n, d = acts[0].shape
    tm = FFN_TILE
    hidden = consts[-5].shape[1]
    row = lambda i: (i, 0)
    mod_spec = pl.BlockSpec((1,) + mods.shape[1:], lambda i: (i // tiles_per_batch, 0, 0))
    in_specs = ([pl.BlockSpec((tm, t.shape[1]), row) for t in acts] + [mod_spec]
                + [_const_spec(t.shape) for t in consts])
    return pl.pallas_call(
        functools.partial(_ffn_kernel, gla_prologue=gla_prologue),
        grid=(n // tm,),
        in_specs=in_specs,
        out_specs=pl.BlockSpec((tm, d), row),
        out_shape=jax.ShapeDtypeStruct((n, d), F32),
        scratch_shapes=[pltpu.VMEM((tm // FFN_SUB, FFN_SUB, hidden), BF16)],
        compiler_params=_params("parallel"),
        name="gla_out_ffn" if gla_prologue else "swiglu_ffn",
    )(*acts, mods, *consts)


def _gmlp_kernel(h_ref, mod_ref, win_ref, vg_ref, vb_ref, ws_ref, bs_ref, wout_ref, lng_ref, lnb_ref,
                 o_ref, v_scr):
    m = mod_ref[0]
    h = h_ref[...]
    tm = h.shape[0]
    width = v_scr.shape[1]
    gw = width // GM_GROUPS
    a = (h * (1.0 + m[1:2]) + m[0:1]).astype(BF16)

    def gelu(z):
        return 0.5 * z * (1.0 + lax.erf(z * (0.5 ** 0.5)))

    part = None
    for lo in range(0, width, GMLP_COLS):
        z = gelu(_dot(a, win_ref[:, width + lo:width + lo + GMLP_COLS]))
        v_scr[:, lo:lo + GMLP_COLS] = z
        part = _lane_fold(z) if part is None else part + _lane_fold(z)
    mu = jnp.sum(part, axis=-1, keepdims=True) * (1.0 / width)
    u_next = _dot(a, win_ref[:, 0:gw])
    part = None
    for lo in range(0, width, GMLP_COLS):
        vc = v_scr[:, lo:lo + GMLP_COLS] - mu
        part = _lane_fold(vc * vc) if part is None else part + _lane_fold(vc * vc)
    rstd = lax.rsqrt(jnp.sum(part, axis=-1, keepdims=True) * (1.0 / width) + 1e-5)
    y = None
    for gi in range(GM_GROUPS):
        cols = slice(gi * gw, (gi + 1) * gw)
        u = gelu(u_next)
        if gi + 1 < GM_GROUPS:
            u_next = _dot(a, win_ref[:, (gi + 1) * gw:(gi + 2) * gw])
        bias = jnp.concatenate([bs_ref[gi]] * (gw // LANES), axis=1)
        gated = []
        for t in range(tm // GM_CHUNK):
            rows = slice(t * GM_CHUNK, (t + 1) * GM_CHUNK)
            vn = (v_scr[rows, cols] - mu[rows]) * rstd[rows] * vg_ref[:, cols] + vb_ref[:, cols]
            s = _dot(ws_ref[gi], vn.astype(BF16)) + bias
            gated.append((u[rows] * s).astype(BF16))
        y_g = _dot(jnp.concatenate(gated, axis=0), wout_ref[cols, :])
        y = y_g if y is None else y + y_g
    res = ALPHA * h + m[2:3] * y
    o_ref[...] = _layer_norm(res, lng_ref[...], lnb_ref[...])


def _gmlp(h2d, mods, consts, tiles_per_batch):
    n, d = h2d.shape
    tm = GMLP_TILE
    width = consts[5].shape[0]
    row = lambda i: (i, 0)
    return pl.pallas_call(
        _gmlp_kernel,
        grid=(n // tm,),
        in_specs=[
            pl.BlockSpec((tm, d), row),
            pl.BlockSpec((1,) + mods.shape[1:], lambda i: (i // tiles_per_batch, 0, 0)),
        ] + [_const_spec(t.shape) for t in consts],
        out_specs=pl.BlockSpec((tm, d), row),
        out_shape=jax.ShapeDtypeStruct((n, d), F32),
        scratch_shapes=[pltpu.VMEM((tm, width), F32)],
        compiler_params=_params("parallel"),
        name="gmlp_mixer",
    )(h2d, mods, *consts)


def kernel(x, c, ctx, c_ctx, mod_w, mod_b, ln_g, ln_b, gla_w_in, gla_w_decay, gla_b_decay, gla_norm_g,
           gla_w_out, gm_w_in, gm_ln_g, gm_ln_b, gm_w_s, gm_b_s, gm_w_out, ffn_w_gate, ffn_w_up, ffn_w_down):
    bsz, l, d = x.shape
    lc = ctx.shape[1]
    n = bsz * l
    assert bsz + 1 <= COND_ROWS
    dk = gla_w_decay.shape[-1]
    dv = gla_w_out.shape[1]
    q_scale = (dk // GLA_HEADS) ** -0.5
    vec = lambda t: t.reshape(1, -1)

    cond = jnp.concatenate([c, c_ctx[None], jnp.zeros((COND_ROWS - bsz - 1, d), F32)], axis=0)
    mods = _adaln(cond, mod_w, mod_b).reshape(DEPTH, COND_ROWS, 6, d)

    w_in = gla_w_in[0]
    o_a = dk + dv
    o_q = o_a + 2 * GLA_RANK
    w_k, w_v, w_a = w_in[:, :dk], w_in[:, dk:o_a], w_in[:, o_a:o_q]
    w_q, w_r = w_in[:, o_q:o_q + dk], w_in[:, o_q + dk:]
    w_a = jnp.pad(w_a, ((0, 0), (0, LANES - 2 * GLA_RANK)))
    w_lat = jnp.concatenate([w_k, w_q, w_a, w_v, w_r], axis=1).astype(BF16)
    w_ctx = jnp.concatenate([w_k, w_a, w_v], axis=1).astype(BF16)
    wdec = jnp.zeros((2, LANES, dk), F32)
    wdec = wdec.at[0, :GLA_RANK].set(gla_w_decay[0, 0]).at[1, GLA_RANK:2 * GLA_RANK].set(gla_w_decay[0, 1])
    wdec = wdec.astype(BF16)
    bdec = gla_b_decay[0].reshape(2, 1, dk)

    tpb = l // PROJ_TILE
    later = (gla_w_out, gm_w_in, gm_w_out, ffn_w_gate, ffn_w_up, ffn_w_down)
    ((v, gate, qin_f, kin_f, kst_f, dec_f, qin_b, kin_b, kst_b, dec_b),
     (w_out, gm_in, gm_out, wg0, wg1, wu0, wu1, wd0, wd1)) = _gla_proj(
        x.reshape(n, d), mods[0], w_lat, wdec, bdec, lambda i: i // tpb, True, PROJ_TILE, dk, dv, q_scale,
        cast=later)
    (vc, kstc_f, decc_f, kstc_b, decc_b), _ = _gla_proj(
        ctx.reshape(bsz * lc, d), mods[0], w_ctx, wdec, bdec, lambda i: bsz, False, lc, dk, dv, q_scale)
    b3 = lambda t: t.reshape(bsz, l, -1)
    v3, vc3 = b3(v), vc.reshape(bsz, lc, dv)
    o_b = _gla_scan(b3(qin_b), b3(kin_b), kst_b, v3, dec_b, kstc_b, vc3, decc_b, True)
    o = _gla_scan(b3(qin_f), b3(kin_f), kst_f, v3, dec_f, kstc_f, vc3, decc_f, False, prev=o_b)

    ffn_w = ((wg0, wu0, wd0), (wg1, wu1, wd1))
    ffn_consts = lambda i: ffn_w[i] + (vec(ln_g[i, 1]), vec(ln_b[i, 1]))
    gla_consts = (vec(gla_norm_g[0]), w_out, vec(ln_g[0, 0]), vec(ln_b[0, 0]))
    tpf = l // FFN_TILE
    h = _ffn((o.reshape(n, dv), gate, x.reshape(n, d)), mods[0], gla_consts + ffn_consts(0), tpf, True)

    bs = jnp.broadcast_to(gm_b_s[0].T[:, :, None], (GM_GROUPS, GM_CHUNK, LANES))
    gm_consts = (gm_in, vec(gm_ln_g[0]), vec(gm_ln_b[0]), gm_w_s[0].astype(BF16), bs,
                 gm_out, vec(ln_g[1, 0]), vec(ln_b[1, 0]))
    h = _gmlp(h, mods[1], gm_consts, l // GMLP_TILE)
    h = _ffn((h,), mods[1], ffn_consts(1), tpf, False)
    return h.reshape(bsz, l, d)
```

```python
import functools

import jax
import jax.numpy as jnp
from jax import lax
from jax.experimental import pallas as pl
from jax.experimental.pallas import tpu as pltpu

F32 = jnp.float32
BF16 = jnp.bfloat16

DEPTH = 2
ALPHA = (2 * DEPTH) ** 0.25
GLA_HEADS = 4
GLA_RANK = 16
GLA_GATE_NORM = 16.0
GLA_CHUNK = 64
GM_GROUPS = 4
GM_CHUNK = 128
COND_ROWS = 8
LANES = 128
VMEM_LIMIT = 56 * 1024 * 1024

PROJ_TILE = 512
SCAN_BLOCK = 1024
DECAY_BLOCK = 256
FFN_TILE = 1024
FFN_SUB = 512
FFN_HEAD_AT = (3, 6)
FFN_TAIL_PIECES = 8
FFN_COLS = 256
GMLP_TILE = 512
GMLP_COLS = 512


def _dot(a, b):
    return jnp.dot(a, b, preferred_element_type=F32)


def _layer_norm(x, g, b, eps=1e-5):
    mu = jnp.mean(x, axis=-1, keepdims=True)
    xc = x - mu
    var = jnp.mean(xc * xc, axis=-1, keepdims=True)
    return xc * lax.rsqrt(var + eps) * g + b


def _silu(x):
    return x * jax.nn.sigmoid(x)


def _lane_fold(t):
    acc = t[:, 0:LANES]
    for j in range(1, t.shape[1] // LANES):
        acc = acc + t[:, j * LANES:(j + 1) * LANES]
    return acc


def _order_after(x, dep):
    bits = pltpu.bitcast(dep[0:x.shape[0], 0:LANES], jnp.uint32)
    half = jnp.uint32(16)
    zero = pltpu.bitcast(lax.shift_right_logical(lax.shift_right_logical(bits, half), half), F32)
    return jnp.concatenate([x[:, 0:LANES] + zero, x[:, LANES:]], axis=1)


def _params(*sem):
    return pltpu.CompilerParams(dimension_semantics=sem, vmem_limit_bytes=VMEM_LIMIT)


def _const_spec(shape):
    nd = len(shape)
    return pl.BlockSpec(shape, lambda *_: (0,) * nd, pipeline_mode=pl.Buffered(1))


def _adaln_kernel(cond_ref, w_ref, b_ref, o_ref):
    s = _silu(cond_ref[...]).astype(BF16)
    o_ref[0] = _dot(s, w_ref[0].astype(BF16)) + b_ref[0]


def _adaln(cond, mod_w, mod_b):
    depth, d, n = mod_w.shape
    tn = n // 4
    return pl.pallas_call(
        _adaln_kernel,
        grid=(depth, n // tn),
        in_specs=[
            pl.BlockSpec((COND_ROWS, d), lambda i, j: (0, 0)),
            pl.BlockSpec((1, d, tn), lambda i, j: (i, 0, j)),
            pl.BlockSpec((1, 1, tn), lambda i, j: (i, 0, j)),
        ],
        out_specs=pl.BlockSpec((1, COND_ROWS, tn), lambda i, j: (i, 0, j)),
        out_shape=jax.ShapeDtypeStruct((depth, COND_ROWS, n), F32),
        compiler_params=_params("arbitrary", "arbitrary"),
        name="adaln",
    )(cond, mod_w, mod_b.reshape(depth, 1, n))


def _log_sigmoid(z):
    return jnp.minimum(z, 0.0) - jnp.log(1.0 + jnp.exp(-jnp.abs(z)))


def _chunk_tri(n, reverse):
    row = lax.broadcasted_iota(jnp.int32, (n, n), 0)
    col = lax.broadcasted_iota(jnp.int32, (n, n), 1)
    shift = GLA_CHUNK.bit_length() - 1
    same = jnp.right_shift(row, shift) == jnp.right_shift(col, shift)
    return jnp.where(same & ((col >= row) if reverse else (col <= row)), 1.0, 0.0).astype(BF16)


def _gla_proj_kernel(x_ref, mod_ref, w_ref, wdec_ref, bdec_ref, *refs, latent, dk, dv, q_scale, cast_layers):
    n_src, n_dst = len(cast_layers), sum(cast_layers)
    dsts = iter(refs[len(refs) - n_dst:])
    for src, layers in zip(refs[:n_src], cast_layers):
        for layer in range(layers):
            next(dsts)[0] = src[layer, 0].astype(BF16)
    out_refs = refs[n_src:len(refs) - n_dst]
    m = mod_ref[0]
    a = (x_ref[...] * (1.0 + m[1:2]) + m[0:1]).astype(BF16)
    tm = a.shape[0]
    nblk = tm // DECAY_BLOCK
    ncb = DECAY_BLOCK // GLA_CHUNK
    o_a = 2 * dk if latent else dk
    o_v = o_a + LANES
    if latent:
        v_ref, gate_ref = out_refs[:2]
        dir_refs = (out_refs[2:6], out_refs[6:10])
    else:
        v_ref = out_refs[0]
        dir_refs = ((None, None) + tuple(out_refs[1:3]), (None, None) + tuple(out_refs[3:5]))
    k = _dot(a, w_ref[:, 0:dk])
    if latent:
        q = _dot(a, w_ref[:, dk:2 * dk]) * q_scale
    a_lr = _dot(a, w_ref[:, o_a:o_a + LANES]).astype(BF16)
    z = [_dot(a_lr, wdec_ref[rev]) + bdec_ref[rev] for rev in (0, 1)]
    v_ref[...] = _dot(a, w_ref[:, o_v:o_v + dv]).astype(BF16)
    g_parts = []
    for rev in (0, 1):
        g = _log_sigmoid(z[rev]) * (1.0 / GLA_GATE_NORM)
        g_hi = g.astype(BF16)
        g_parts.append((g_hi, (g - g_hi.astype(F32)).astype(BF16)))
    tris = (_chunk_tri(DECAY_BLOCK, False), _chunk_tri(DECAY_BLOCK, True))
    b = {}
    for rev in (0, 1):
        for blk in range(nblk):
            rows = slice(blk * DECAY_BLOCK, (blk + 1) * DECAY_BLOCK)
            b[rev, blk] = _dot(tris[rev], g_parts[rev][0][rows]) + _dot(tris[rev], g_parts[rev][1][rows])
    if latent:
        gate_ref[...] = _silu(_dot(a, w_ref[:, o_v + dv:o_v + 2 * dv])).astype(BF16)
    for rev in (0, 1):
        qin_ref, kin_ref, kst_ref, dec_ref = dir_refs[rev]
        group_chunks = dec_ref.shape[1]
        for blk in range(nblk):
            rows = slice(blk * DECAY_BLOCK, (blk + 1) * DECAY_BLOCK)
            bb = b[rev, blk]
            if latent:
                qin_ref[rows, :] = (q[rows] * jnp.exp(bb)).astype(BF16)
                kin_ref[rows, :] = (k[rows] * jnp.exp(-bb)).astype(BF16)
            ends = []
            for c in range(ncb):
                lo = c * GLA_CHUNK
                b_c = bb[lo:lo + GLA_CHUNK]
                b_end = b_c[0:1] if rev else b_c[GLA_CHUNK - 1:GLA_CHUNK]
                kst = k[blk * DECAY_BLOCK + lo:blk * DECAY_BLOCK + lo + GLA_CHUNK] * jnp.exp(b_end - b_c)
                kst_ref[blk * ncb + c] = kst.T.astype(BF16)
                ends.append(b_end)
            first = blk * ncb
            dec_ref[first // group_chunks, first % group_chunks:first % group_chunks + ncb, :] = jnp.exp(
                jnp.concatenate(ends, axis=0))


def _gla_proj(x2d, mods, w, wdec, bdec, mod_row, latent, tm, dk, dv, q_scale, cast=()):
    n, d = x2d.shape
    steps = n // tm
    group = min(SCAN_BLOCK, tm)
    row = lambda i: (i, 0)
    row3 = lambda i: (i, 0, 0)
    slabs = [t.reshape(t.shape[0], steps, t.shape[1] // steps, t.shape[2]) for t in cast]
    slab_in = [pl.BlockSpec((t.shape[0], 1) + t.shape[2:], lambda i: (0, i, 0, 0)) for t in slabs]
    slab_out = [(jax.ShapeDtypeStruct(t.shape[1:], BF16), pl.BlockSpec((1,) + t.shape[2:], row3))
                for t in slabs for _ in range(t.shape[0])]
    tok = lambda c: (jax.ShapeDtypeStruct((n, c), BF16), pl.BlockSpec((tm, c), row))
    kst = (jax.ShapeDtypeStruct((n // GLA_CHUNK, dk, GLA_CHUNK), BF16),
           pl.BlockSpec((tm // GLA_CHUNK, dk, GLA_CHUNK), row3))
    dec = (jax.ShapeDtypeStruct((n // group, group // GLA_CHUNK, dk), F32),
           pl.BlockSpec((tm // group, group // GLA_CHUNK, dk), row3))
    per_dir = [tok(dk), tok(dk), kst, dec] if latent else [kst, dec]
    outs = ([tok(dv), tok(dv)] if latent else [tok(dv)]) + per_dir + per_dir
    res = pl.pallas_call(
        functools.partial(_gla_proj_kernel, latent=latent, dk=dk, dv=dv, q_scale=q_scale,
                          cast_layers=tuple(t.shape[0] for t in cast)),
        grid=(steps,),
        in_specs=[
            pl.BlockSpec((tm, d), row),
            pl.BlockSpec((1,) + mods.shape[1:], lambda i: (mod_row(i), 0, 0)),
            _const_spec(w.shape), _const_spec(wdec.shape), _const_spec(bdec.shape),
        ] + slab_in,
        out_specs=[o[1] for o in outs] + [o[1] for o in slab_out],
        out_shape=[o[0] for o in outs] + [o[0] for o in slab_out],
        compiler_params=_params("parallel"),
        name="gla_in_proj" if latent else "gla_ctx_proj",
    )(x2d, mods, w, wdec, bdec, *slabs)
    n_out = len(outs)
    shapes = [t.shape[1:] for t in cast for _ in range(t.shape[0])]
    return res[:n_out], [r.reshape(s) for r, s in zip(res[n_out:], shapes)]


def _decay_columns(dec_row, dvh):
    dkh = dec_row.shape[1]
    dcol = jnp.broadcast_to(dec_row, (dkh, dkh)).T
    return jnp.concatenate([dcol] * (dvh // dkh), axis=1)


def _gla_scan_kernel(*refs, reverse, add_prev):
    qin_ref, kin_ref, kst_ref, v_ref, dec_ref, kstc_ref, vc_ref, decc_ref = refs[:8]
    if add_prev:
        prev_ref, o_ref, s_ref = refs[8:]
    else:
        o_ref, s_ref = refs[8:]
    dkh, dvh = s_ref.shape[1:]
    heads = range(GLA_HEADS)
    ks = [slice(h * dkh, (h + 1) * dkh) for h in heads]
    vs = [slice(h * dvh, (h + 1) * dvh) for h in heads]

    def order(nc):
        return list(range(nc - 1, -1, -1) if reverse else range(nc))

    def rows(c):
        return slice(c * GLA_CHUNK, (c + 1) * GLA_CHUNK)

    @pl.when(pl.program_id(1) == 0)
    def _():
        chunks = order(vc_ref.shape[1] // GLA_CHUNK)
        upd = {(c, h): _dot(kstc_ref[c, ks[h], :], vc_ref[0, rows(c), vs[h]]) for c in chunks for h in heads}
        for h in heads:
            s = jnp.zeros((dkh, dvh), F32)
            for c in chunks:
                s = s * _decay_columns(decc_ref[0, c:c + 1, ks[h]], dvh) + upd[c, h]
            s_ref[h] = s

    chunks = order(v_ref.shape[1] // GLA_CHUNK)
    pairs = [(c, h) for c in chunks for h in heads]
    r64 = lax.broadcasted_iota(jnp.int32, (GLA_CHUNK, GLA_CHUNK), 0)
    c64 = lax.broadcasted_iota(jnp.int32, (GLA_CHUNK, GLA_CHUNK), 1)
    mask = (c64 >= r64) if reverse else (c64 <= r64)
    att = {(c, h): lax.dot_general(qin_ref[0, rows(c), ks[h]], kin_ref[0, rows(c), ks[h]],
                                   (((1,), (1,)), ((), ())), preferred_element_type=F32) for c, h in pairs}
    upd = {(c, h): _dot(kst_ref[c, ks[h], :], v_ref[0, rows(c), vs[h]]) for c, h in pairs}
    dcol = {(c, h): _decay_columns(dec_ref[0, c:c + 1, ks[h]], dvh) for c, h in pairs}
    lhs = {(c, h): jnp.concatenate([qin_ref[0, rows(c), ks[h]], jnp.where(mask, att[c, h], 0.0).astype(BF16)],
                                   axis=1) for c, h in pairs}
    state = [s_ref[h] for h in heads]
    for c in chunks:
        outs = []
        for h in heads:
            rhs = jnp.concatenate([state[h].astype(BF16), v_ref[0, rows(c), vs[h]]], axis=0)
            outs.append(_dot(lhs[c, h], rhs))
            state[h] = state[h] * dcol[c, h] + upd[c, h]
        o = jnp.concatenate(outs, axis=1)
        if add_prev:
            o = o + prev_ref[0, rows(c), :].astype(F32)
        o_ref[0, rows(c), :] = o.astype(o_ref.dtype)
    for h in heads:
        s_ref[h] = state[h]


def _gla_scan(qin, kin, kst, v, dec, kstc, vc, decc, reverse, prev=None):
    bsz, l, dk = qin.shape
    dv = v.shape[2]
    lc = vc.shape[1]
    tb = SCAN_BLOCK
    nblk = l // tb
    ncb = tb // GLA_CHUNK

    def pos(j):
        return (nblk - 1 - j) if reverse else j

    blk = lambda b, j: (b, pos(j), 0)
    flat = lambda b, j: (b * nblk + pos(j), 0, 0)
    ctx = lambda b, j: (b, 0, 0)
    in_specs = [
        pl.BlockSpec((1, tb, dk), blk), pl.BlockSpec((1, tb, dk), blk),
        pl.BlockSpec((ncb, dk, GLA_CHUNK), flat), pl.BlockSpec((1, tb, dv), blk),
        pl.BlockSpec((1, ncb, dk), flat),
        pl.BlockSpec((lc // GLA_CHUNK, dk, GLA_CHUNK), ctx), pl.BlockSpec((1, lc, dv), ctx),
        pl.BlockSpec((1, lc // GLA_CHUNK, dk), ctx),
    ]
    args = [qin, kin, kst, v, dec.reshape(bsz * nblk, ncb, dk), kstc, vc, decc]
    if prev is not None:
        in_specs.append(pl.BlockSpec((1, tb, dv), blk))
        args.append(prev)
    return pl.pallas_call(
        functools.partial(_gla_scan_kernel, reverse=reverse, add_prev=prev is not None),
        grid=(bsz, nblk),
        in_specs=in_specs,
        out_specs=pl.BlockSpec((1, tb, dv), blk),
        out_shape=jax.ShapeDtypeStruct((bsz, l, dv), BF16),
        scratch_shapes=[pltpu.VMEM((GLA_HEADS, dk // GLA_HEADS, dv // GLA_HEADS), F32)],
        compiler_params=_params("arbitrary", "arbitrary"),
        name="gla_scan_bwd" if reverse else "gla_scan_fwd",
    )(*args)


def _ffn_kernel(*refs, gla_prologue):
    if gla_prologue:
        (o_ref, gate_ref, x_ref, mod_ref, ng_ref, wout_ref, lng0_ref, lnb0_ref,
         wg_ref, wu_ref, wd_ref, lng_ref, lnb_ref, out_ref, hid_ref) = refs
    else:
        h_ref, mod_ref, wg_ref, wu_ref, wd_ref, lng_ref, lnb_ref, out_ref, hid_ref = refs
    m = mod_ref[0]
    hidden = wg_ref.shape[1]
    nsub = out_ref.shape[0] // FFN_SUB

    def head(lo, size):
        rows = slice(lo, lo + size)
        if gla_prologue:
            o = o_ref[rows, :].astype(F32)
            dvh = o.shape[1] // GLA_HEADS
            normed = []
            for hd in range(GLA_HEADS):
                o_h = o[:, hd * dvh:(hd + 1) * dvh]
                ms = jnp.mean(o_h * o_h, axis=-1, keepdims=True)
                normed.append(o_h * lax.rsqrt(ms + 1e-6) * ng_ref[...])
            gated = (jnp.concatenate(normed, axis=1) * gate_ref[rows, :].astype(F32)).astype(BF16)
            y = _dot(gated, wout_ref[...])
            h = _layer_norm(ALPHA * x_ref[rows, :] + m[2:3] * y, lng0_ref[...], lnb0_ref[...])
        else:
            h = h_ref[rows, :]
        af = h * (1.0 + m[4:5]) + m[3:4]
        return h, af.astype(BF16), _lane_fold(af)

    def tail(lo, h, f, after=None):
        res = ALPHA * h + m[5:6] * f
        if after is not None:
            res = _order_after(res, after)
        out_ref[lo:lo + h.shape[0], :] = _layer_norm(res, lng_ref[...], lnb_ref[...])

    head_rows = FFN_SUB // len(FFN_HEAD_AT)
    tail_rows = FFN_SUB // FFN_TAIL_PIECES
    h, a, _ = head(0, FFN_SUB)
    pending = None
    for sub in range(nsub):
        nxt, done_tail = [], 0
        for j, lo in enumerate(range(0, hidden, FFN_COLS)):
            piece = None
            if sub + 1 < nsub and j in FFN_HEAD_AT:
                piece = head((sub + 1) * FFN_SUB + FFN_HEAD_AT.index(j) * head_rows, head_rows)
                nxt.append(piece)
            g = _dot(a, wg_ref[:, lo:lo + FFN_COLS])
            u = _dot(a, wu_ref[:, lo:lo + FFN_COLS])
            prod = _silu(g) * u
            if piece is not None:
                prod = jnp.concatenate([_order_after(prod[0:head_rows], piece[2]), prod[head_rows:]], axis=0)
            hid_ref[sub, :, lo:lo + FFN_COLS] = prod.astype(BF16)
            if pending is not None and piece is None and done_tail < FFN_TAIL_PIECES:
                p_lo, p_h, p_f = pending
                r = slice(done_tail * tail_rows, (done_tail + 1) * tail_rows)
                tail(p_lo + done_tail * tail_rows, p_h[r], p_f[r], after=g)
                done_tail += 1
        assert pending is None or done_tail == FFN_TAIL_PIECES
        pending = (sub * FFN_SUB, h, _dot(hid_ref[sub], wd_ref[...]))
        if nxt:
            h = jnp.concatenate([p[0] for p in nxt], axis=0)
            a = jnp.concatenate([p[1] for p in nxt], axis=0)
    tail(*pending)


def _ffn(acts, mods, consts, tiles_per_batch, gla_prologue):
    n, d = acts[0].shape
    tm = FFN_TILE
    hidden = consts[-5].shape[1]
    row = lambda i: (i, 0)
    mod_spec = pl.BlockSpec((1,) + mods.shape[1:], lambda i: (i // tiles_per_batch, 0, 0))
    in_specs = ([pl.BlockSpec((tm, t.shape[1]), row) for t in acts] + [mod_spec]
                + [_const_spec(t.shape) for t in consts])
    return pl.pallas_call(
        functools.partial(_ffn_kernel, gla_prologue=gla_prologue),
        grid=(n // tm,),
        in_specs=in_specs,
        out_specs=pl.BlockSpec((tm, d), row),
        out_shape=jax.ShapeDtypeStruct((n, d), F32),
        scratch_shapes=[pltpu.VMEM((tm // FFN_SUB, FFN_SUB, hidden), BF16)],
        compiler_params=_params("parallel"),
        name="gla_out_ffn" if gla_prologue else "swiglu_ffn",
    )(*acts, mods, *consts)


def _gmlp_kernel(h_ref, mod_ref, win_ref, vg_ref, vb_ref, ws_ref, bs_ref, wout_ref, lng_ref, lnb_ref,
                 o_ref, v_scr):
    m = mod_ref[0]
    h = h_ref[...]
    tm = h.shape[0]
    width = v_scr.shape[1]
    gw = width // GM_GROUPS
    a = (h * (1.0 + m[1:2]) + m[0:1]).astype(BF16)

    def gelu(z):
        return 0.5 * z * (1.0 + lax.erf(z * (0.5 ** 0.5)))

    part = None
    for lo in range(0, width, GMLP_COLS):
        z = gelu(_dot(a, win_ref[:, width + lo:width + lo + GMLP_COLS]))
        v_scr[:, lo:lo + GMLP_COLS] = z
        part = _lane_fold(z) if part is None else part + _lane_fold(z)
    mu = jnp.sum(part, axis=-1, keepdims=True) * (1.0 / width)
    u_next = _dot(a, win_ref[:, 0:gw])
    part = None
    for lo in range(0, width, GMLP_COLS):
        vc = v_scr[:, lo:lo + GMLP_COLS] - mu
        part = _lane_fold(vc * vc) if part is None else part + _lane_fold(vc * vc)
    rstd = lax.rsqrt(jnp.sum(part, axis=-1, keepdims=True) * (1.0 / width) + 1e-5)
    y = None
    for gi in range(GM_GROUPS):
        cols = slice(gi * gw, (gi + 1) * gw)
        u = gelu(u_next)
        if gi + 1 < GM_GROUPS:
            u_next = _dot(a, win_ref[:, (gi + 1) * gw:(gi + 2) * gw])
        bias = jnp.concatenate([bs_ref[gi]] * (gw // LANES), axis=1)
        gated = []
        for t in range(tm // GM_CHUNK):
            rows = slice(t * GM_CHUNK, (t + 1) * GM_CHUNK)
            vn = (v_scr[rows, cols] - mu[rows]) * rstd[rows] * vg_ref[:, cols] + vb_ref[:, cols]
            s = _dot(ws_ref[gi], vn.astype(BF16)) + bias
            gated.append((u[rows] * s).astype(BF16))
        y_g = _dot(jnp.concatenate(gated, axis=0), wout_ref[cols, :])
        y = y_g if y is None else y + y_g
    res = ALPHA * h + m[2:3] * y
    o_ref[...] = _layer_norm(res, lng_ref[...], lnb_ref[...])


def _gmlp(h2d, mods, consts, tiles_per_batch):
    n, d = h2d.shape
    tm = GMLP_TILE
    width = consts[5].shape[0]
    row = lambda i: (i, 0)
    return pl.pallas_call(
        _gmlp_kernel,
        grid=(n // tm,),
        in_specs=[
            pl.BlockSpec((tm, d), row),
            pl.BlockSpec((1,) + mods.shape[1:], lambda i: (i // tiles_per_batch, 0, 0)),
        ] + [_const_spec(t.shape) for t in consts],
        out_specs=pl.BlockSpec((tm, d), row),
        out_shape=jax.ShapeDtypeStruct((n, d), F32),
        scratch_shapes=[pltpu.VMEM((tm, width), F32)],
        compiler_params=_params("parallel"),
        name="gmlp_mixer",
    )(h2d, mods, *consts)


def kernel(x, c, ctx, c_ctx, mod_w, mod_b, ln_g, ln_b, gla_w_in, gla_w_decay, gla_b_decay, gla_norm_g,
           gla_w_out, gm_w_in, gm_ln_g, gm_ln_b, gm_w_s, gm_b_s, gm_w_out, ffn_w_gate, ffn_w_up, ffn_w_down):
    bsz, l, d = x.shape
    lc = ctx.shape[1]
    n = bsz * l
    assert bsz + 1 <= COND_ROWS
    dk = gla_w_decay.shape[-1]
    dv = gla_w_out.shape[1]
    q_scale = (dk // GLA_HEADS) ** -0.5
    vec = lambda t: t.reshape(1, -1)

    cond = jnp.concatenate([c, c_ctx[None], jnp.zeros((COND_ROWS - bsz - 1, d), F32)], axis=0)
    mods = _adaln(cond, mod_w, mod_b).reshape(DEPTH, COND_ROWS, 6, d)

    w_in = gla_w_in[0]
    o_a = dk + dv
    o_q = o_a + 2 * GLA_RANK
    w_k, w_v, w_a = w_in[:, :dk], w_in[:, dk:o_a], w_in[:, o_a:o_q]
    w_q, w_r = w_in[:, o_q:o_q + dk], w_in[:, o_q + dk:]
    w_a = jnp.pad(w_a, ((0, 0), (0, LANES - 2 * GLA_RANK)))
    w_lat = jnp.concatenate([w_k, w_q, w_a, w_v, w_r], axis=1).astype(BF16)
    w_ctx = jnp.concatenate([w_k, w_a, w_v], axis=1).astype(BF16)
    wdec = jnp.zeros((2, LANES, dk), F32)
    wdec = wdec.at[0, :GLA_RANK].set(gla_w_decay[0, 0]).at[1, GLA_RANK:2 * GLA_RANK].set(gla_w_decay[0, 1])
    wdec = wdec.astype(BF16)
    bdec = gla_b_decay[0].reshape(2, 1, dk)

    tpb = l // PROJ_TILE
    later = (gla_w_out, gm_w_in, gm_w_out, ffn_w_gate, ffn_w_up, ffn_w_down)
    ((v, gate, qin_f, kin_f, kst_f, dec_f, qin_b, kin_b, kst_b, dec_b),
     (w_out, gm_in, gm_out, wg0, wg1, wu0, wu1, wd0, wd1)) = _gla_proj(
        x.reshape(n, d), mods[0], w_lat, wdec, bdec, lambda i: i // tpb, True, PROJ_TILE, dk, dv, q_scale,
        cast=later)
    (vc, kstc_f, decc_f, kstc_b, decc_b), _ = _gla_proj(
        ctx.reshape(bsz * lc, d), mods[0], w_ctx, wdec, bdec, lambda i: bsz, False, lc, dk, dv, q_scale)
    b3 = lambda t: t.reshape(bsz, l, -1)
    v3, vc3 = b3(v), vc.reshape(bsz, lc, dv)
    o_b = _gla_scan(b3(qin_b), b3(kin_b), kst_b, v3, dec_b, kstc_b, vc3, decc_b, True)
    o = _gla_scan(b3(qin_f), b3(kin_f), kst_f, v3, dec_f, kstc_f, vc3, decc_f, False, prev=o_b)

    ffn_w = ((wg0, wu0, wd0), (wg1, wu1, wd1))
    ffn_consts = lambda i: ffn_w[i] + (vec(ln_g[i, 1]), vec(ln_b[i, 1]))
    gla_consts = (vec(gla_norm_g[0]), w_out, vec(ln_g[0, 0]), vec(ln_b[0, 0]))
    tpf = l // FFN_TILE
    h = _ffn((o.reshape(n, dv), gate, x.reshape(n, d)), mods[0], gla_consts + ffn_consts(0), tpf, True)

    bs = jnp.broadcast_to(gm_b_s[0].T[:, :, None], (GM_GROUPS, GM_CHUNK, LANES))
    gm_consts = (gm_in, vec(gm_ln_g[0]), vec(gm_ln_b[0]), gm_w_s[0].astype(BF16), bs,
                 gm_out, vec(ln_g[1, 0]), vec(ln_b[1, 0]))
    h = _gmlp(h, mods[1], gm_consts, l // GMLP_TILE)
    h = _ffn((h,), mods[1], ffn_consts(1), tpf, False)
    return h.reshape(bsz, l, d)
```

```python
import functools

import jax
import jax.numpy as jnp
from jax import lax
from jax.experimental import pallas as pl
from jax.experimental.pallas import tpu as pltpu

F32 = jnp.float32
BF16 = jnp.bfloat16

DEPTH = 2
ALPHA = (2 * DEPTH) ** 0.25
GLA_HEADS = 4
GLA_RANK = 16
GLA_GATE_NORM = 16.0
GLA_CHUNK = 64
GM_GROUPS = 4
GM_CHUNK = 128
COND_ROWS = 8
LANES = 128
VMEM_LIMIT = 56 * 1024 * 1024

PROJ_TILE = 512
SCAN_BLOCK = 1024
DECAY_BLOCK = 256
KST_GROUP = 2
FFN_TILE = 1024
FFN_SUB = 512
FFN_HEAD_AT = (3, 6)
FFN_TAIL_PIECES = 8
FFN_COLS = 256
GMLP_TILE = 512
GMLP_COLS = 512


def _dot(a, b):
    return jnp.dot(a, b, preferred_element_type=F32)


def _layer_norm(x, g, b, eps=1e-5):
    mu = jnp.mean(x, axis=-1, keepdims=True)
    xc = x - mu
    var = jnp.mean(xc * xc, axis=-1, keepdims=True)
    return xc * lax.rsqrt(var + eps) * g + b


def _silu(x):
    return x * jax.nn.sigmoid(x)


def _lane_fold(t):
    acc = t[:, 0:LANES]
    for j in range(1, t.shape[1] // LANES):
        acc = acc + t[:, j * LANES:(j + 1) * LANES]
    return acc


def _order_after(x, dep):
    bits = pltpu.bitcast(dep[0:x.shape[0], 0:LANES], jnp.uint32)
    half = jnp.uint32(16)
    zero = pltpu.bitcast(lax.shift_right_logical(lax.shift_right_logical(bits, half), half), F32)
    return jnp.concatenate([x[:, 0:LANES] + zero, x[:, LANES:]], axis=1)


def _params(*sem):
    return pltpu.CompilerParams(dimension_semantics=sem, vmem_limit_bytes=VMEM_LIMIT)


def _const_spec(shape):
    nd = len(shape)
    return pl.BlockSpec(shape, lambda *_: (0,) * nd, pipeline_mode=pl.Buffered(1))


def _adaln_kernel(cond_ref, w_ref, b_ref, o_ref):
    s = _silu(cond_ref[...]).astype(BF16)
    o_ref[0] = _dot(s, w_ref[0].astype(BF16)) + b_ref[0]


def _adaln(cond, mod_w, mod_b):
    depth, d, n = mod_w.shape
    tn = n // 4
    return pl.pallas_call(
        _adaln_kernel,
        grid=(depth, n // tn),
        in_specs=[
            pl.BlockSpec((COND_ROWS, d), lambda i, j: (0, 0)),
            pl.BlockSpec((1, d, tn), lambda i, j: (i, 0, j)),
            pl.BlockSpec((1, 1, tn), lambda i, j: (i, 0, j)),
        ],
        out_specs=pl.BlockSpec((1, COND_ROWS, tn), lambda i, j: (i, 0, j)),
        out_shape=jax.ShapeDtypeStruct((depth, COND_ROWS, n), F32),
        compiler_params=_params("arbitrary", "arbitrary"),
        name="adaln",
    )(cond, mod_w, mod_b.reshape(depth, 1, n))


def _log_sigmoid(z):
    return jnp.minimum(z, 0.0) - jnp.log(1.0 + jnp.exp(-jnp.abs(z)))


def _chunk_tri(n, reverse):
    row = lax.broadcasted_iota(jnp.int32, (n, n), 0)
    col = lax.broadcasted_iota(jnp.int32, (n, n), 1)
    shift = GLA_CHUNK.bit_length() - 1
    same = jnp.right_shift(row, shift) == jnp.right_shift(col, shift)
    return jnp.where(same & ((col >= row) if reverse else (col <= row)), 1.0, 0.0).astype(BF16)


def _gla_proj_kernel(x_ref, mod_ref, w_ref, wdec_ref, bdec_ref, *refs, latent, dk, dv, q_scale, cast_layers):
    n_src, n_dst = len(cast_layers), sum(cast_layers)
    dsts = iter(refs[len(refs) - n_dst:])
    for src, layers in zip(refs[:n_src], cast_layers):
        for layer in range(layers):
            next(dsts)[0] = src[layer, 0].astype(BF16)
    out_refs = refs[n_src:len(refs) - n_dst]
    m = mod_ref[0]
    a = (x_ref[...] * (1.0 + m[1:2]) + m[0:1]).astype(BF16)
    tm = a.shape[0]
    nblk = tm // DECAY_BLOCK
    ncb = DECAY_BLOCK // GLA_CHUNK
    o_a = 2 * dk if latent else dk
    o_v = o_a + LANES
    if latent:
        v_ref, gate_ref = out_refs[:2]
        dir_refs = (out_refs[2:6], out_refs[6:10])
    else:
        v_ref = out_refs[0]
        dir_refs = ((None, None) + tuple(out_refs[1:3]), (None, None) + tuple(out_refs[3:5]))
    k = _dot(a, w_ref[:, 0:dk])
    if latent:
        q = _dot(a, w_ref[:, dk:2 * dk]) * q_scale
    a_lr = _dot(a, w_ref[:, o_a:o_a + LANES]).astype(BF16)
    z = [_dot(a_lr, wdec_ref[rev]) + bdec_ref[rev] for rev in (0, 1)]
    v_ref[...] = _dot(a, w_ref[:, o_v:o_v + dv]).astype(BF16)
    g_parts = []
    for rev in (0, 1):
        g = _log_sigmoid(z[rev]) * (1.0 / GLA_GATE_NORM)
        g_hi = g.astype(BF16)
        g_parts.append((g_hi, (g - g_hi.astype(F32)).astype(BF16)))
    tris = (_chunk_tri(DECAY_BLOCK, False), _chunk_tri(DECAY_BLOCK, True))
    b = {}
    for rev in (0, 1):
        for blk in range(nblk):
            rows = slice(blk * DECAY_BLOCK, (blk + 1) * DECAY_BLOCK)
            b[rev, blk] = _dot(tris[rev], g_parts[rev][0][rows]) + _dot(tris[rev], g_parts[rev][1][rows])
    if latent:
        gate_ref[...] = _silu(_dot(a, w_ref[:, o_v + dv:o_v + 2 * dv])).astype(BF16)
    for rev in (0, 1):
        qin_ref, kin_ref, kst_ref, dec_ref = dir_refs[rev]
        group_chunks = dec_ref.shape[1]
        for blk in range(nblk):
            rows = slice(blk * DECAY_BLOCK, (blk + 1) * DECAY_BLOCK)
            bb = b[rev, blk]
            if latent:
                qin_ref[rows, :] = (q[rows] * jnp.exp(bb)).astype(BF16)
                kin_ref[rows, :] = (k[rows] * jnp.exp(-bb)).astype(BF16)
            ends, kst = [], []
            for c in range(ncb):
                lo = c * GLA_CHUNK
                b_c = bb[lo:lo + GLA_CHUNK]
                b_end = b_c[0:1] if rev else b_c[GLA_CHUNK - 1:GLA_CHUNK]
                kst.append(k[blk * DECAY_BLOCK + lo:blk * DECAY_BLOCK + lo + GLA_CHUNK] * jnp.exp(b_end - b_c))
                ends.append(b_end)
            for p in range(ncb // KST_GROUP):
                grp = jnp.concatenate(kst[p * KST_GROUP:(p + 1) * KST_GROUP], axis=0)
                kst_ref[blk * (ncb // KST_GROUP) + p] = grp.T.astype(BF16)
            first = blk * ncb
            dec_ref[first // group_chunks, first % group_chunks:first % group_chunks + ncb, :] = jnp.exp(
                jnp.concatenate(ends, axis=0))


def _gla_proj(x2d, mods, w, wdec, bdec, mod_row, latent, tm, dk, dv, q_scale, cast=()):
    n, d = x2d.shape
    steps = n // tm
    group = min(SCAN_BLOCK, tm)
    row = lambda i: (i, 0)
    row3 = lambda i: (i, 0, 0)
    slabs = [t.reshape(t.shape[0], steps, t.shape[1] // steps, t.shape[2]) for t in cast]
    slab_in = [pl.BlockSpec((t.shape[0], 1) + t.shape[2:], lambda i: (0, i, 0, 0)) for t in slabs]
    slab_out = [(jax.ShapeDtypeStruct(t.shape[1:], BF16), pl.BlockSpec((1,) + t.shape[2:], row3))
                for t in slabs for _ in range(t.shape[0])]
    tok = lambda c: (jax.ShapeDtypeStruct((n, c), BF16), pl.BlockSpec((tm, c), row))
    kst = (jax.ShapeDtypeStruct((n // LANES, dk, LANES), BF16),
           pl.BlockSpec((tm // LANES, dk, LANES), row3))
    dec = (jax.ShapeDtypeStruct((n // group, group // GLA_CHUNK, dk), F32),
           pl.BlockSpec((tm // group, group // GLA_CHUNK, dk), row3))
    per_dir = [tok(dk), tok(dk), kst, dec] if latent else [kst, dec]
    outs = ([tok(dv), tok(dv)] if latent else [tok(dv)]) + per_dir + per_dir
    res = pl.pallas_call(
        functools.partial(_gla_proj_kernel, latent=latent, dk=dk, dv=dv, q_scale=q_scale,
                          cast_layers=tuple(t.shape[0] for t in cast)),
        grid=(steps,),
        in_specs=[
            pl.BlockSpec((tm, d), row),
            pl.BlockSpec((1,) + mods.shape[1:], lambda i: (mod_row(i), 0, 0)),
            _const_spec(w.shape), _const_spec(wdec.shape), _const_spec(bdec.shape),
        ] + slab_in,
        out_specs=[o[1] for o in outs] + [o[1] for o in slab_out],
        out_shape=[o[0] for o in outs] + [o[0] for o in slab_out],
        compiler_params=_params("parallel"),
        name="gla_in_proj" if latent else "gla_ctx_proj",
    )(x2d, mods, w, wdec, bdec, *slabs)
    n_out = len(outs)
    shapes = [t.shape[1:] for t in cast for _ in range(t.shape[0])]
    return res[:n_out], [r.reshape(s) for r, s in zip(res[n_out:], shapes)]


def _decay_columns(dec_row, dvh):
    dkh = dec_row.shape[1]
    dcol = jnp.broadcast_to(dec_row, (dkh, dkh)).T
    return jnp.concatenate([dcol] * (dvh // dkh), axis=1)


def _gla_scan_kernel(*refs, reverse, add_prev):
    qin_ref, kin_ref, kst_ref, v_ref, dec_ref, kstc_ref, vc_ref, decc_ref = refs[:8]
    if add_prev:
        prev_ref, o_ref, s_ref = refs[8:]
    else:
        o_ref, s_ref = refs[8:]
    dkh, dvh = s_ref.shape[1:]
    heads = range(GLA_HEADS)
    ks = [slice(h * dkh, (h + 1) * dkh) for h in heads]
    vs = [slice(h * dvh, (h + 1) * dvh) for h in heads]

    def order(nc):
        return list(range(nc - 1, -1, -1) if reverse else range(nc))

    def rows(c):
        return slice(c * GLA_CHUNK, (c + 1) * GLA_CHUNK)

    lane_chunk = lax.broadcasted_iota(jnp.int32, (dkh, LANES), 1) // GLA_CHUNK

    def state_update(kst_g_ref, v_g_ref, c, h):
        g = c // KST_GROUP
        keys = jnp.where(lane_chunk == c % KST_GROUP, kst_g_ref[g, ks[h], :], jnp.zeros((), BF16))
        return _dot(keys, v_g_ref[0, g * LANES:(g + 1) * LANES, vs[h]])

    @pl.when(pl.program_id(1) == 0)
    def _():
        chunks = order(vc_ref.shape[1] // GLA_CHUNK)
        upd = {(c, h): state_update(kstc_ref, vc_ref, c, h) for c in chunks for h in heads}
        for h in heads:
            s = jnp.zeros((dkh, dvh), F32)
            for c in chunks:
                s = s * _decay_columns(decc_ref[0, c:c + 1, ks[h]], dvh) + upd[c, h]
            s_ref[h] = s

    chunks = order(v_ref.shape[1] // GLA_CHUNK)
    pairs = [(c, h) for c in chunks for h in heads]
    r64 = lax.broadcasted_iota(jnp.int32, (GLA_CHUNK, GLA_CHUNK), 0)
    c64 = lax.broadcasted_iota(jnp.int32, (GLA_CHUNK, GLA_CHUNK), 1)
    mask = (c64 >= r64) if reverse else (c64 <= r64)
    att = {(c, h): lax.dot_general(qin_ref[0, rows(c), ks[h]], kin_ref[0, rows(c), ks[h]],
                                   (((1,), (1,)), ((), ())), preferred_element_type=F32) for c, h in pairs}
    upd = {(c, h): state_update(kst_ref, v_ref, c, h) for c, h in pairs}
    dcol = {(c, h): _decay_columns(dec_ref[0, c:c + 1, ks[h]], dvh) for c, h in pairs}
    lhs = {(c, h): jnp.concatenate([qin_ref[0, rows(c), ks[h]], jnp.where(mask, att[c, h], 0.0).astype(BF16)],
                                   axis=1) for c, h in pairs}
    state = [s_ref[h] for h in heads]
    for c in chunks:
        outs = []
        for h in heads:
            rhs = jnp.concatenate([state[h].astype(BF16), v_ref[0, rows(c), vs[h]]], axis=0)
            outs.append(_dot(lhs[c, h], rhs))
            state[h] = state[h] * dcol[c, h] + upd[c, h]
        o = jnp.concatenate(outs, axis=1)
        if add_prev:
            o = o + prev_ref[0, rows(c), :].astype(F32)
        o_ref[0, rows(c), :] = o.astype(o_ref.dtype)
    for h in heads:
        s_ref[h] = state[h]


def _gla_scan(qin, kin, kst, v, dec, kstc, vc, decc, reverse, prev=None):
    bsz, l, dk = qin.shape
    dv = v.shape[2]
    lc = vc.shape[1]
    tb = SCAN_BLOCK
    nblk = l // tb
    ncb = tb // GLA_CHUNK

    def pos(j):
        return (nblk - 1 - j) if reverse else j

    blk = lambda b, j: (b, pos(j), 0)
    flat = lambda b, j: (b * nblk + pos(j), 0, 0)
    ctx = lambda b, j: (b, 0, 0)
    in_specs = [
        pl.BlockSpec((1, tb, dk), blk), pl.BlockSpec((1, tb, dk), blk),
        pl.BlockSpec((tb // LANES, dk, LANES), flat), pl.BlockSpec((1, tb, dv), blk),
        pl.BlockSpec((1, ncb, dk), flat),
        pl.BlockSpec((lc // LANES, dk, LANES), ctx), pl.BlockSpec((1, lc, dv), ctx),
        pl.BlockSpec((1, lc // GLA_CHUNK, dk), ctx),
    ]
    args = [qin, kin, kst, v, dec.reshape(bsz * nblk, ncb, dk), kstc, vc, decc]
    if prev is not None:
        in_specs.append(pl.BlockSpec((1, tb, dv), blk))
        args.append(prev)
    return pl.pallas_call(
        functools.partial(_gla_scan_kernel, reverse=reverse, add_prev=prev is not None),
        grid=(bsz, nblk),
        in_specs=in_specs,
        out_specs=pl.BlockSpec((1, tb, dv), blk),
        out_shape=jax.ShapeDtypeStruct((bsz, l, dv), BF16),
        scratch_shapes=[pltpu.VMEM((GLA_HEADS, dk // GLA_HEADS, dv // GLA_HEADS), F32)],
        compiler_params=_params("arbitrary", "arbitrary"),
        name="gla_scan_bwd" if reverse else "gla_scan_fwd",
    )(*args)


def _ffn_kernel(*refs, gla_prologue):
    if gla_prologue:
        (o_ref, gate_ref, x_ref, mod_ref, ng_ref, wout_ref, lng0_ref, lnb0_ref,
         wg_ref, wu_ref, wd_ref, lng_ref, lnb_ref, out_ref, hid_ref) = refs
    else:
        h_ref, mod_ref, wg_ref, wu_ref, wd_ref, lng_ref, lnb_ref, out_ref, hid_ref = refs
    m = mod_ref[0]
    hidden = wg_ref.shape[1]
    nsub = out_ref.shape[0] // FFN_SUB

    def head(lo, size):
        rows = slice(lo, lo + size)
        if gla_prologue:
            o = o_ref[rows, :].astype(F32)
            dvh = o.shape[1] // GLA_HEADS
            normed = []
            for hd in range(GLA_HEADS):
                o_h = o[:, hd * dvh:(hd + 1) * dvh]
                ms = jnp.mean(o_h * o_h, axis=-1, keepdims=True)
                normed.append(o_h * lax.rsqrt(ms + 1e-6) * ng_ref[...])
            gated = (jnp.concatenate(normed, axis=1) * gate_ref[rows, :].astype(F32)).astype(BF16)
            y = _dot(gated, wout_ref[...])
            h = _layer_norm(ALPHA * x_ref[rows, :] + m[2:3] * y, lng0_ref[...], lnb0_ref[...])
        else:
            h = h_ref[rows, :]
        af = h * (1.0 + m[4:5]) + m[3:4]
        return h, af.astype(BF16), _lane_fold(af)

    def tail(lo, h, f, after=None):
        res = ALPHA * h + m[5:6] * f
        if after is not None:
            res = _order_after(res, after)
        out_ref[lo:lo + h.shape[0], :] = _layer_norm(res, lng_ref[...], lnb_ref[...])

    head_rows = FFN_SUB // len(FFN_HEAD_AT)
    tail_rows = FFN_SUB // FFN_TAIL_PIECES
    h, a, _ = head(0, FFN_SUB)
    pending = None
    for sub in range(nsub):
        nxt, done_tail = [], 0
        for j, lo in enumerate(range(0, hidden, FFN_COLS)):
            piece = None
            if sub + 1 < nsub and j in FFN_HEAD_AT:
                piece = head((sub + 1) * FFN_SUB + FFN_HEAD_AT.index(j) * head_rows, head_rows)
                nxt.append(piece)
            g = _dot(a, wg_ref[:, lo:lo + FFN_COLS])
            u = _dot(a, wu_ref[:, lo:lo + FFN_COLS])
            prod = _silu(g) * u
            if piece is not None:
                prod = jnp.concatenate([_order_after(prod[0:head_rows], piece[2]), prod[head_rows:]], axis=0)
            hid_ref[sub, :, lo:lo + FFN_COLS] = prod.astype(BF16)
            if pending is not None and piece is None and done_tail < FFN_TAIL_PIECES:
                p_lo, p_h, p_f = pending
                r = slice(done_tail * tail_rows, (done_tail + 1) * tail_rows)
                tail(p_lo + done_tail * tail_rows, p_h[r], p_f[r], after=g)
                done_tail += 1
        assert pending is None or done_tail == FFN_TAIL_PIECES
        pending = (sub * FFN_SUB, h, _dot(hid_ref[sub], wd_ref[...]))
        if nxt:
            h = jnp.concatenate([p[0] for p in nxt], axis=0)
            a = jnp.concatenate([p[1] for p in nxt], axis=0)
    tail(*pending)


def _ffn(acts, mods, consts, tiles_per_batch, gla_prologue):
    n, d = acts[0].shape
    tm = FFN_TILE
    hidden = consts[-5].shape[1]
    row = lambda i: (i, 0)
    mod_spec = pl.BlockSpec((1,) + mods.shape[1:], lambda i: (i // tiles_per_batch, 0, 0))
    in_specs = ([pl.BlockSpec((tm, t.shape[1]), row) for t in acts] + [mod_spec]
                + [_const_spec(t.shape) for t in consts])
    return pl.pallas_call(
        functools.partial(_ffn_kernel, gla_prologue=gla_prologue),
        grid=(n // tm,),
        in_specs=in_specs,
        out_specs=pl.BlockSpec((tm, d), row),
        out_shape=jax.ShapeDtypeStruct((n, d), F32),
        scratch_shapes=[pltpu.VMEM((tm // FFN_SUB, FFN_SUB, hidden), BF16)],
        compiler_params=_params("parallel"),
        name="gla_out_ffn" if gla_prologue else "swiglu_ffn",
    )(*acts, mods, *consts)


def _gmlp_kernel(h_ref, mod_ref, win_ref, vg_ref, vb_ref, ws_ref, bs_ref, wout_ref, lng_ref, lnb_ref,
                 o_ref, v_scr):
    m = mod_ref[0]
    h = h_ref[...]
    tm = h.shape[0]
    width = v_scr.shape[1]
    gw = width // GM_GROUPS
    a = (h * (1.0 + m[1:2]) + m[0:1]).astype(BF16)

    def gelu(z):
        return 0.5 * z * (1.0 + lax.erf(z * (0.5 ** 0.5)))

    part = None
    for lo in range(0, width, GMLP_COLS):
        z = gelu(_dot(a, win_ref[:, width + lo:width + lo + GMLP_COLS]))
        v_scr[:, lo:lo + GMLP_COLS] = z
        part = _lane_fold(z) if part is None else part + _lane_fold(z)
    mu = jnp.sum(part, axis=-1, keepdims=True) * (1.0 / width)
    u_next = _dot(a, win_ref[:, 0:gw])
    part = None
    for lo in range(0, width, GMLP_COLS):
        vc = v_scr[:, lo:lo + GMLP_COLS] - mu
        part = _lane_fold(vc * vc) if part is None else part + _lane_fold(vc * vc)
    rstd = lax.rsqrt(jnp.sum(part, axis=-1, keepdims=True) * (1.0 / width) + 1e-5)
    y = None
    for gi in range(GM_GROUPS):
        cols = slice(gi * gw, (gi + 1) * gw)
        u = gelu(u_next)
        if gi + 1 < GM_GROUPS:
            u_next = _dot(a, win_ref[:, (gi + 1) * gw:(gi + 2) * gw])
        bias = jnp.concatenate([bs_ref[gi]] * (gw // LANES), axis=1)
        gated = []
        for t in range(tm // GM_CHUNK):
            rows = slice(t * GM_CHUNK, (t + 1) * GM_CHUNK)
            vn = (v_scr[rows, cols] - mu[rows]) * rstd[rows] * vg_ref[:, cols] + vb_ref[:, cols]
            s = _dot(ws_ref[gi], vn.astype(BF16)) + bias
            gated.append((u[rows] * s).astype(BF16))
        y_g = _dot(jnp.concatenate(gated, axis=0), wout_ref[cols, :])
        y = y_g if y is None else y + y_g
    res = ALPHA * h + m[2:3] * y
    o_ref[...] = _layer_norm(res, lng_ref[...], lnb_ref[...])


def _gmlp(h2d, mods, consts, tiles_per_batch):
    n, d = h2d.shape
    tm = GMLP_TILE
    width = consts[5].shape[0]
    row = lambda i: (i, 0)
    return pl.pallas_call(
        _gmlp_kernel,
        grid=(n // tm,),
        in_specs=[
            pl.BlockSpec((tm, d), row),
            pl.BlockSpec((1,) + mods.shape[1:], lambda i: (i // tiles_per_batch, 0, 0)),
        ] + [_const_spec(t.shape) for t in consts],
        out_specs=pl.BlockSpec((tm, d), row),
        out_shape=jax.ShapeDtypeStruct((n, d), F32),
        scratch_shapes=[pltpu.VMEM((tm, width), F32)],
        compiler_params=_params("parallel"),
        name="gmlp_mixer",
    )(h2d, mods, *consts)


def kernel(x, c, ctx, c_ctx, mod_w, mod_b, ln_g, ln_b, gla_w_in, gla_w_decay, gla_b_decay, gla_norm_g,
           gla_w_out, gm_w_in, gm_ln_g, gm_ln_b, gm_w_s, gm_b_s, gm_w_out, ffn_w_gate, ffn_w_up, ffn_w_down):
    bsz, l, d = x.shape
    lc = ctx.shape[1]
    n = bsz * l
    assert bsz + 1 <= COND_ROWS
    dk = gla_w_decay.shape[-1]
    dv = gla_w_out.shape[1]
    q_scale = (dk // GLA_HEADS) ** -0.5
    vec = lambda t: t.reshape(1, -1)

    cond = jnp.concatenate([c, c_ctx[None], jnp.zeros((COND_ROWS - bsz - 1, d), F32)], axis=0)
    mods = _adaln(cond, mod_w, mod_b).reshape(DEPTH, COND_ROWS, 6, d)

    w_in = gla_w_in[0]
    o_a = dk + dv
    o_q = o_a + 2 * GLA_RANK
    w_k, w_v, w_a = w_in[:, :dk], w_in[:, dk:o_a], w_in[:, o_a:o_q]
    w_q, w_r = w_in[:, o_q:o_q + dk], w_in[:, o_q + dk:]
    w_a = jnp.pad(w_a, ((0, 0), (0, LANES - 2 * GLA_RANK)))
    w_lat = jnp.concatenate([w_k, w_q, w_a, w_v, w_r], axis=1).astype(BF16)
    w_ctx = jnp.concatenate([w_k, w_a, w_v], axis=1).astype(BF16)
    wdec = jnp.zeros((2, LANES, dk), F32)
    wdec = wdec.at[0, :GLA_RANK].set(gla_w_decay[0, 0]).at[1, GLA_RANK:2 * GLA_RANK].set(gla_w_decay[0, 1])
    wdec = wdec.astype(BF16)
    bdec = gla_b_decay[0].reshape(2, 1, dk)

    tpb = l // PROJ_TILE
    later = (gla_w_out, gm_w_in, gm_w_out, ffn_w_gate, ffn_w_up, ffn_w_down)
    ((v, gate, qin_f, kin_f, kst_f, dec_f, qin_b, kin_b, kst_b, dec_b),
     (w_out, gm_in, gm_out, wg0, wg1, wu0, wu1, wd0, wd1)) = _gla_proj(
        x.reshape(n, d), mods[0], w_lat, wdec, bdec, lambda i: i // tpb, True, PROJ_TILE, dk, dv, q_scale,
        cast=later)
    (vc, kstc_f, decc_f, kstc_b, decc_b), _ = _gla_proj(
        ctx.reshape(bsz * lc, d), mods[0], w_ctx, wdec, bdec, lambda i: bsz, False, lc, dk, dv, q_scale)
    b3 = lambda t: t.reshape(bsz, l, -1)
    v3, vc3 = b3(v), vc.reshape(bsz, lc, dv)
    o_b = _gla_scan(b3(qin_b), b3(kin_b), kst_b, v3, dec_b, kstc_b, vc3, decc_b, True)
    o = _gla_scan(b3(qin_f), b3(kin_f), kst_f, v3, dec_f, kstc_f, vc3, decc_f, False, prev=o_b)

    ffn_w = ((wg0, wu0, wd0), (wg1, wu1, wd1))
    ffn_consts = lambda i: ffn_w[i] + (vec(ln_g[i, 1]), vec(ln_b[i, 1]))
    gla_consts = (vec(gla_norm_g[0]), w_out, vec(ln_g[0, 0]), vec(ln_b[0, 0]))
    tpf = l // FFN_TILE
    h = _ffn((o.reshape(n, dv), gate, x.reshape(n, d)), mods[0], gla_consts + ffn_consts(0), tpf, True)

    bs = jnp.broadcast_to(gm_b_s[0].T[:, :, None], (GM_GROUPS, GM_CHUNK, LANES))
    gm_consts = (gm_in, vec(gm_ln_g[0]), vec(gm_ln_b[0]), gm_w_s[0].astype(BF16), bs,
                 gm_out, vec(ln_g[1, 0]), vec(ln_b[1, 0]))
    h = _gmlp(h, mods[1], gm_consts, l // GMLP_TILE)
    h = _ffn((h,), mods[1], ffn_consts(1), tpf, False)
    return h.reshape(bsz, l, d)
```

```python
import functools

import jax
import jax.numpy as jnp
from jax import lax
from jax.experimental import pallas as pl
from jax.experimental.pallas import tpu as pltpu

F32 = jnp.float32
BF16 = jnp.bfloat16

DEPTH = 2
ALPHA = (2 * DEPTH) ** 0.25
GLA_HEADS = 4
GLA_RANK = 16
GLA_GATE_NORM = 16.0
GLA_CHUNK = 64
GM_GROUPS = 4
GM_CHUNK = 128
COND_ROWS = 8
LANES = 128
VMEM_LIMIT = 56 * 1024 * 1024

PROJ_TILE = 512
SCAN_BLOCK = 1024
DECAY_BLOCK = 256
KST_GROUP = 2
FFN_TILE = 1024
FFN_SUB = 512
FFN_HEAD_AT = (3, 6)
FFN_TAIL_PIECES = 8
FFN_COLS = 256
GMLP_TILE = 512
GMLP_COLS = 512


def _dot(a, b):
    return jnp.dot(a, b, preferred_element_type=F32)


def _layer_norm(x, g, b, eps=1e-5):
    mu = jnp.mean(x, axis=-1, keepdims=True)
    xc = x - mu
    var = jnp.mean(xc * xc, axis=-1, keepdims=True)
    return xc * lax.rsqrt(var + eps) * g + b


def _silu(x):
    return x * jax.nn.sigmoid(x)


def _lane_fold(t):
    acc = t[:, 0:LANES]
    for j in range(1, t.shape[1] // LANES):
        acc = acc + t[:, j * LANES:(j + 1) * LANES]
    return acc


def _order_after(x, dep):
    bits = pltpu.bitcast(dep[0:x.shape[0], 0:LANES], jnp.uint32)
    half = jnp.uint32(16)
    zero = pltpu.bitcast(lax.shift_right_logical(lax.shift_right_logical(bits, half), half), F32)
    return jnp.concatenate([x[:, 0:LANES] + zero, x[:, LANES:]], axis=1)


def _params(*sem):
    return pltpu.CompilerParams(dimension_semantics=sem, vmem_limit_bytes=VMEM_LIMIT)


def _const_spec(shape):
    nd = len(shape)
    return pl.BlockSpec(shape, lambda *_: (0,) * nd, pipeline_mode=pl.Buffered(1))


def _adaln_kernel(cond_ref, w_ref, b_ref, win_ref, wdecay_ref, o_ref, wlat_ref, wctx_ref, wdec_ref, *, dk, dv):
    s = _silu(cond_ref[...]).astype(BF16)
    o_ref[0] = _dot(s, w_ref[0].astype(BF16)) + b_ref[0]
    wdec_ref[...] = jnp.zeros_like(wdec_ref)
    for rev in (0, 1):
        wdec_ref[rev, rev * GLA_RANK:(rev + 1) * GLA_RANK, :] = wdecay_ref[rev].astype(BF16)
    w = win_ref[...]
    o_a = dk + dv
    o_q = o_a + 2 * GLA_RANK
    k, v, q, r = w[:, 0:dk], w[:, dk:o_a], w[:, o_q:o_q + dk], w[:, o_q + dk:o_q + dk + dv]
    a = jnp.concatenate([w[:, o_a:o_q], jnp.zeros((w.shape[0], LANES - 2 * GLA_RANK), F32)], axis=1)
    wlat_ref[...] = jnp.concatenate([k, q, a, v, r], axis=1).astype(BF16)
    wctx_ref[...] = jnp.concatenate([k, a, v], axis=1).astype(BF16)


def _adaln(cond, mod_w, mod_b, w_in, w_decay, dk, dv):
    depth, d, n = mod_w.shape
    nj = 4
    tn = n // nj
    rows = w_in.shape[0] // (depth * nj)
    slab = lambda i, j: (i * nj + j, 0)
    whole = lambda i, j: (0, 0, 0)
    return pl.pallas_call(
        functools.partial(_adaln_kernel, dk=dk, dv=dv),
        grid=(depth, nj),
        in_specs=[
            pl.BlockSpec((COND_ROWS, d), lambda i, j: (0, 0)),
            pl.BlockSpec((1, d, tn), lambda i, j: (i, 0, j)),
            pl.BlockSpec((1, 1, tn), lambda i, j: (i, 0, j)),
            pl.BlockSpec((rows, w_in.shape[1]), slab),
            pl.BlockSpec(w_decay.shape, whole),
        ],
        out_specs=[pl.BlockSpec((1, COND_ROWS, tn), lambda i, j: (i, 0, j)),
                   pl.BlockSpec((rows, 2 * dk + LANES + 2 * dv), slab),
                   pl.BlockSpec((rows, dk + LANES + dv), slab),
                   pl.BlockSpec((2, LANES, dk), whole)],
        out_shape=[jax.ShapeDtypeStruct((depth, COND_ROWS, n), F32),
                   jax.ShapeDtypeStruct((w_in.shape[0], 2 * dk + LANES + 2 * dv), BF16),
                   jax.ShapeDtypeStruct((w_in.shape[0], dk + LANES + dv), BF16),
                   jax.ShapeDtypeStruct((2, LANES, dk), BF16)],
        compiler_params=_params("arbitrary", "arbitrary"),
        name="adaln",
    )(cond, mod_w, mod_b.reshape(depth, 1, n), w_in, w_decay)


def _log_sigmoid(z):
    return jnp.minimum(z, 0.0) - jnp.log(1.0 + jnp.exp(-jnp.abs(z)))


def _chunk_tri(n, reverse):
    row = lax.broadcasted_iota(jnp.int32, (n, n), 0)
    col = lax.broadcasted_iota(jnp.int32, (n, n), 1)
    shift = GLA_CHUNK.bit_length() - 1
    same = jnp.right_shift(row, shift) == jnp.right_shift(col, shift)
    return jnp.where(same & ((col >= row) if reverse else (col <= row)), 1.0, 0.0).astype(BF16)


def _gla_proj_kernel(x_ref, mod_ref, w_ref, wdec_ref, bdec_ref, *refs, latent, dk, dv, q_scale, cast_layers):
    n_src, n_dst = len(cast_layers), sum(cast_layers)
    dsts = iter(refs[len(refs) - n_dst:])
    for src, layers in zip(refs[:n_src], cast_layers):
        for layer in range(layers):
            next(dsts)[0] = src[layer, 0].astype(BF16)
    out_refs = refs[n_src:len(refs) - n_dst]
    m = mod_ref[0]
    a = (x_ref[...] * (1.0 + m[1:2]) + m[0:1]).astype(BF16)
    tm = a.shape[0]
    nblk = tm // DECAY_BLOCK
    ncb = DECAY_BLOCK // GLA_CHUNK
    o_a = 2 * dk if latent else dk
    o_v = o_a + LANES
    if latent:
        v_ref, gate_ref = out_refs[:2]
        dir_refs = (out_refs[2:6], out_refs[6:10])
    else:
        v_ref = out_refs[0]
        dir_refs = ((None, None) + tuple(out_refs[1:3]), (None, None) + tuple(out_refs[3:5]))
    k = _dot(a, w_ref[:, 0:dk])
    if latent:
        q = _dot(a, w_ref[:, dk:2 * dk]) * q_scale
    a_lr = _dot(a, w_ref[:, o_a:o_a + LANES]).astype(BF16)
    z = [_dot(a_lr, wdec_ref[rev]) + bdec_ref[rev] for rev in (0, 1)]
    v_ref[...] = _dot(a, w_ref[:, o_v:o_v + dv]).astype(BF16)
    g_parts = []
    for rev in (0, 1):
        g = _log_sigmoid(z[rev]) * (1.0 / GLA_GATE_NORM)
        g_hi = g.astype(BF16)
        g_parts.append((g_hi, (g - g_hi.astype(F32)).astype(BF16)))
    tris = (_chunk_tri(DECAY_BLOCK, False), _chunk_tri(DECAY_BLOCK, True))
    b = {}
    for rev in (0, 1):
        for blk in range(nblk):
            rows = slice(blk * DECAY_BLOCK, (blk + 1) * DECAY_BLOCK)
            b[rev, blk] = _dot(tris[rev], g_parts[rev][0][rows]) + _dot(tris[rev], g_parts[rev][1][rows])
    if latent:
        gate_ref[...] = _silu(_dot(a, w_ref[:, o_v + dv:o_v + 2 * dv])).astype(BF16)
    for rev in (0, 1):
        qin_ref, kin_ref, kst_ref, dec_ref = dir_refs[rev]
        group_chunks = dec_ref.shape[1]
        for blk in range(nblk):
            rows = slice(blk * DECAY_BLOCK, (blk + 1) * DECAY_BLOCK)
            bb = b[rev, blk]
            if latent:
                qin_ref[rows, :] = (q[rows] * jnp.exp(bb)).astype(BF16)
                kin_ref[rows, :] = (k[rows] * jnp.exp(-bb)).astype(BF16)
            ends, kst = [], []
            for c in range(ncb):
                lo = c * GLA_CHUNK
                b_c = bb[lo:lo + GLA_CHUNK]
                b_end = b_c[0:1] if rev else b_c[GLA_CHUNK - 1:GLA_CHUNK]
                kst.append(k[blk * DECAY_BLOCK + lo:blk * DECAY_BLOCK + lo + GLA_CHUNK] * jnp.exp(b_end - b_c))
                ends.append(b_end)
            for p in range(ncb // KST_GROUP):
                grp = jnp.concatenate(kst[p * KST_GROUP:(p + 1) * KST_GROUP], axis=0)
                kst_ref[blk * (ncb // KST_GROUP) + p] = grp.T.astype(BF16)
            first = blk * ncb
            dec_ref[first // group_chunks, first % group_chunks:first % group_chunks + ncb, :] = jnp.exp(
                jnp.concatenate(ends, axis=0))


def _gla_proj(x2d, mods, w, wdec, bdec, mod_row, latent, tm, dk, dv, q_scale, cast=()):
    n, d = x2d.shape
    steps = n // tm
    group = min(SCAN_BLOCK, tm)
    row = lambda i: (i, 0)
    row3 = lambda i: (i, 0, 0)
    slabs = [t.reshape(t.shape[0], steps, t.shape[1] // steps, t.shape[2]) for t in cast]
    slab_in = [pl.BlockSpec((t.shape[0], 1) + t.shape[2:], lambda i: (0, i, 0, 0)) for t in slabs]
    slab_out = [(jax.ShapeDtypeStruct(t.shape[1:], BF16), pl.BlockSpec((1,) + t.shape[2:], row3))
                for t in slabs for _ in range(t.shape[0])]
    tok = lambda c: (jax.ShapeDtypeStruct((n, c), BF16), pl.BlockSpec((tm, c), row))
    kst = (jax.ShapeDtypeStruct((n // LANES, dk, LANES), BF16),
           pl.BlockSpec((tm // LANES, dk, LANES), row3))
    dec = (jax.ShapeDtypeStruct((n // group, group // GLA_CHUNK, dk), F32),
           pl.BlockSpec((tm // group, group // GLA_CHUNK, dk), row3))
    per_dir = [tok(dk), tok(dk), kst, dec] if latent else [kst, dec]
    outs = ([tok(dv), tok(dv)] if latent else [tok(dv)]) + per_dir + per_dir
    res = pl.pallas_call(
        functools.partial(_gla_proj_kernel, latent=latent, dk=dk, dv=dv, q_scale=q_scale,
                          cast_layers=tuple(t.shape[0] for t in cast)),
        grid=(steps,),
        in_specs=[
            pl.BlockSpec((tm, d), row),
            pl.BlockSpec((1,) + mods.shape[1:], lambda i: (mod_row(i), 0, 0)),
            _const_spec(w.shape), _const_spec(wdec.shape), _const_spec(bdec.shape),
        ] + slab_in,
        out_specs=[o[1] for o in outs] + [o[1] for o in slab_out],
        out_shape=[o[0] for o in outs] + [o[0] for o in slab_out],
        compiler_params=_params("parallel"),
        name="gla_in_proj" if latent else "gla_ctx_proj",
    )(x2d, mods, w, wdec, bdec, *slabs)
    n_out = len(outs)
    shapes = [t.shape[1:] for t in cast for _ in range(t.shape[0])]
    return res[:n_out], [r.reshape(s) for r, s in zip(res[n_out:], shapes)]


def _decay_columns(dec_row, dvh):
    dkh = dec_row.shape[1]
    dcol = jnp.broadcast_to(dec_row, (dkh, dkh)).T
    return jnp.concatenate([dcol] * (dvh // dkh), axis=1)


def _gla_scan_kernel(*refs, reverse, add_prev):
    qin_ref, kin_ref, kst_ref, v_ref, dec_ref, kstc_ref, vc_ref, decc_ref = refs[:8]
    if add_prev:
        prev_ref, o_ref, s_ref = refs[8:]
    else:
        o_ref, s_ref = refs[8:]
    dkh, dvh = s_ref.shape[1:]
    heads = range(GLA_HEADS)
    ks = [slice(h * dkh, (h + 1) * dkh) for h in heads]
    vs = [slice(h * dvh, (h + 1) * dvh) for h in heads]

    def order(nc):
        return list(range(nc - 1, -1, -1) if reverse else range(nc))

    def rows(c):
        return slice(c * GLA_CHUNK, (c + 1) * GLA_CHUNK)

    lane_chunk = lax.broadcasted_iota(jnp.int32, (dkh, LANES), 1) // GLA_CHUNK

    def state_update(kst_g_ref, v_g_ref, c, h):
        g = c // KST_GROUP
        keys = jnp.where(lane_chunk == c % KST_GROUP, kst_g_ref[g, ks[h], :], jnp.zeros((), BF16))
        return _dot(keys, v_g_ref[0, g * LANES:(g + 1) * LANES, vs[h]])

    @pl.when(pl.program_id(1) == 0)
    def _():
        chunks = order(vc_ref.shape[1] // GLA_CHUNK)
        upd = {(c, h): state_update(kstc_ref, vc_ref, c, h) for c in chunks for h in heads}
        for h in heads:
            s = jnp.zeros((dkh, dvh), F32)
            for c in chunks:
                s = s * _decay_columns(decc_ref[0, c:c + 1, ks[h]], dvh) + upd[c, h]
            s_ref[h] = s

    chunks = order(v_ref.shape[1] // GLA_CHUNK)
    pairs = [(c, h) for c in chunks for h in heads]
    r64 = lax.broadcasted_iota(jnp.int32, (GLA_CHUNK, GLA_CHUNK), 0)
    c64 = lax.broadcasted_iota(jnp.int32, (GLA_CHUNK, GLA_CHUNK), 1)
    mask = (c64 >= r64) if reverse else (c64 <= r64)
    att = {(c, h): lax.dot_general(qin_ref[0, rows(c), ks[h]], kin_ref[0, rows(c), ks[h]],
                                   (((1,), (1,)), ((), ())), preferred_element_type=F32) for c, h in pairs}
    upd = {(c, h): state_update(kst_ref, v_ref, c, h) for c, h in pairs}
    dcol = {(c, h): _decay_columns(dec_ref[0, c:c + 1, ks[h]], dvh) for c, h in pairs}
    lhs = {(c, h): jnp.concatenate([qin_ref[0, rows(c), ks[h]], jnp.where(mask, att[c, h], 0.0).astype(BF16)],
                                   axis=1) for c, h in pairs}
    state = [s_ref[h] for h in heads]
    for c in chunks:
        outs = []
        for h in heads:
            rhs = jnp.concatenate([state[h].astype(BF16), v_ref[0, rows(c), vs[h]]], axis=0)
            outs.append(_dot(lhs[c, h], rhs))
            state[h] = state[h] * dcol[c, h] + upd[c, h]
        o = jnp.concatenate(outs, axis=1)
        if add_prev:
            o = o + prev_ref[0, rows(c), :].astype(F32)
        o_ref[0, rows(c), :] = o.astype(o_ref.dtype)
    for h in heads:
        s_ref[h] = state[h]


def _gla_scan(qin, kin, kst, v, dec, kstc, vc, decc, reverse, prev=None):
    bsz, l, dk = qin.shape
    dv = v.shape[2]
    lc = vc.shape[1]
    tb = SCAN_BLOCK
    nblk = l // tb
    ncb = tb // GLA_CHUNK

    def pos(j):
        return (nblk - 1 - j) if reverse else j

    blk = lambda b, j: (b, pos(j), 0)
    flat = lambda b, j: (b * nblk + pos(j), 0, 0)
    ctx = lambda b, j: (b, 0, 0)
    in_specs = [
        pl.BlockSpec((1, tb, dk), blk), pl.BlockSpec((1, tb, dk), blk),
        pl.BlockSpec((tb // LANES, dk, LANES), flat), pl.BlockSpec((1, tb, dv), blk),
        pl.BlockSpec((1, ncb, dk), flat),
        pl.BlockSpec((lc // LANES, dk, LANES), ctx), pl.BlockSpec((1, lc, dv), ctx),
        pl.BlockSpec((1, lc // GLA_CHUNK, dk), ctx),
    ]
    args = [qin, kin, kst, v, dec.reshape(bsz * nblk, ncb, dk), kstc, vc, decc]
    if prev is not None:
        in_specs.append(pl.BlockSpec((1, tb, dv), blk))
        args.append(prev)
    return pl.pallas_call(
        functools.partial(_gla_scan_kernel, reverse=reverse, add_prev=prev is not None),
        grid=(bsz, nblk),
        in_specs=in_specs,
        out_specs=pl.BlockSpec((1, tb, dv), blk),
        out_shape=jax.ShapeDtypeStruct((bsz, l, dv), BF16),
        scratch_shapes=[pltpu.VMEM((GLA_HEADS, dk // GLA_HEADS, dv // GLA_HEADS), F32)],
        compiler_params=_params("arbitrary", "arbitrary"),
        name="gla_scan_bwd" if reverse else "gla_scan_fwd",
    )(*args)


def _ffn_kernel(*refs, gla_prologue):
    if gla_prologue:
        (o_ref, gate_ref, x_ref, mod_ref, ng_ref, wout_ref, lng0_ref, lnb0_ref,
         wg_ref, wu_ref, wd_ref, lng_ref, lnb_ref, out_ref, hid_ref) = refs
    else:
        h_ref, mod_ref, wg_ref, wu_ref, wd_ref, lng_ref, lnb_ref, out_ref, hid_ref = refs
    m = mod_ref[0]
    hidden = wg_ref.shape[1]
    nsub = out_ref.shape[0] // FFN_SUB

    def head(lo, size):
        rows = slice(lo, lo + size)
        if gla_prologue:
            o = o_ref[rows, :].astype(F32)
            dvh = o.shape[1] // GLA_HEADS
            normed = []
            for hd in range(GLA_HEADS):
                o_h = o[:, hd * dvh:(hd + 1) * dvh]
                ms = jnp.mean(o_h * o_h, axis=-1, keepdims=True)
                normed.append(o_h * lax.rsqrt(ms + 1e-6) * ng_ref[...])
            gated = (jnp.concatenate(normed, axis=1) * gate_ref[rows, :].astype(F32)).astype(BF16)
            y = _dot(gated, wout_ref[...])
            h = _layer_norm(ALPHA * x_ref[rows, :] + m[2:3] * y, lng0_ref[...], lnb0_ref[...])
        else:
            h = h_ref[rows, :]
        af = h * (1.0 + m[4:5]) + m[3:4]
        return h, af.astype(BF16), _lane_fold(af)

    def tail(lo, h, f, after=None):
        res = ALPHA * h + m[5:6] * f
        if after is not None:
            res = _order_after(res, after)
        out_ref[lo:lo + h.shape[0], :] = _layer_norm(res, lng_ref[...], lnb_ref[...])

    head_rows = FFN_SUB // len(FFN_HEAD_AT)
    tail_rows = FFN_SUB // FFN_TAIL_PIECES
    h, a, _ = head(0, FFN_SUB)
    pending = None
    for sub in range(nsub):
        nxt, done_tail = [], 0
        for j, lo in enumerate(range(0, hidden, FFN_COLS)):
            piece = None
            if sub + 1 < nsub and j in FFN_HEAD_AT:
                piece = head((sub + 1) * FFN_SUB + FFN_HEAD_AT.index(j) * head_rows, head_rows)
                nxt.append(piece)
            g = _dot(a, wg_ref[:, lo:lo + FFN_COLS])
            u = _dot(a, wu_ref[:, lo:lo + FFN_COLS])
            prod = _silu(g) * u
            if piece is not None:
                prod = jnp.concatenate([_order_after(prod[0:head_rows], piece[2]), prod[head_rows:]], axis=0)
            hid_ref[sub, :, lo:lo + FFN_COLS] = prod.astype(BF16)
            if pending is not None and piece is None and done_tail < FFN_TAIL_PIECES:
                p_lo, p_h, p_f = pending
                r = slice(done_tail * tail_rows, (done_tail + 1) * tail_rows)
                tail(p_lo + done_tail * tail_rows, p_h[r], p_f[r], after=g)
                done_tail += 1
        assert pending is None or done_tail == FFN_TAIL_PIECES
        pending = (sub * FFN_SUB, h, _dot(hid_ref[sub], wd_ref[...]))
        if nxt:
            h = jnp.concatenate([p[0] for p in nxt], axis=0)
            a = jnp.concatenate([p[1] for p in nxt], axis=0)
    tail(*pending)


def _ffn(acts, mods, consts, tiles_per_batch, gla_prologue):
    n, d = acts[0].shape
    tm = FFN_TILE
    hidden = consts[-5].shape[1]
    row = lambda i: (i, 0)
    mod_spec = pl.BlockSpec((1,) + mods.shape[1:], lambda i: (i // tiles_per_batch, 0, 0))
    in_specs = ([pl.BlockSpec((tm, t.shape[1]), row) for t in acts] + [mod_spec]
                + [_const_spec(t.shape) for t in consts])
    return pl.pallas_call(
        functools.partial(_ffn_kernel, gla_prologue=gla_prologue),
        grid=(n // tm,),
        in_specs=in_specs,
        out_specs=pl.BlockSpec((tm, d), row),
        out_shape=jax.ShapeDtypeStruct((n, d), F32),
        scratch_shapes=[pltpu.VMEM((tm // FFN_SUB, FFN_SUB, hidden), BF16)],
        compiler_params=_params("parallel"),
        name="gla_out_ffn" if gla_prologue else "swiglu_ffn",
    )(*acts, mods, *consts)


def _gmlp_kernel(h_ref, mod_ref, win_ref, vg_ref, vb_ref, ws_ref, bs_ref, wout_ref, lng_ref, lnb_ref,
                 o_ref, v_scr):
    m = mod_ref[0]
    h = h_ref[...]
    tm = h.shape[0]
    width = v_scr.shape[1]
    gw = width // GM_GROUPS
    a = (h * (1.0 + m[1:2]) + m[0:1]).astype(BF16)

    def gelu(z):
        return 0.5 * z * (1.0 + lax.erf(z * (0.5 ** 0.5)))

    part = None
    for lo in range(0, width, GMLP_COLS):
        z = gelu(_dot(a, win_ref[:, width + lo:width + lo + GMLP_COLS]))
        v_scr[:, lo:lo + GMLP_COLS] = z
        part = _lane_fold(z) if part is None else part + _lane_fold(z)
    mu = jnp.sum(part, axis=-1, keepdims=True) * (1.0 / width)
    u_next = _dot(a, win_ref[:, 0:gw])
    part = None
    for lo in range(0, width, GMLP_COLS):
        vc = v_scr[:, lo:lo + GMLP_COLS] - mu
        part = _lane_fold(vc * vc) if part is None else part + _lane_fold(vc * vc)
    rstd = lax.rsqrt(jnp.sum(part, axis=-1, keepdims=True) * (1.0 / width) + 1e-5)
    y = None
    for gi in range(GM_GROUPS):
        cols = slice(gi * gw, (gi + 1) * gw)
        u = gelu(u_next)
        if gi + 1 < GM_GROUPS:
            u_next = _dot(a, win_ref[:, (gi + 1) * gw:(gi + 2) * gw])
        bias = jnp.concatenate([bs_ref[gi]] * (gw // LANES), axis=1)
        gated = []
        for t in range(tm // GM_CHUNK):
            rows = slice(t * GM_CHUNK, (t + 1) * GM_CHUNK)
            vn = (v_scr[rows, cols] - mu[rows]) * rstd[rows] * vg_ref[:, cols] + vb_ref[:, cols]
            s = _dot(ws_ref[gi], vn.astype(BF16)) + bias
            gated.append((u[rows] * s).astype(BF16))
        y_g = _dot(jnp.concatenate(gated, axis=0), wout_ref[cols, :])
        y = y_g if y is None else y + y_g
    res = ALPHA * h + m[2:3] * y
    o_ref[...] = _layer_norm(res, lng_ref[...], lnb_ref[...])


def _gmlp(h2d, mods, consts, tiles_per_batch):
    n, d = h2d.shape
    tm = GMLP_TILE
    width = consts[5].shape[0]
    row = lambda i: (i, 0)
    return pl.pallas_call(
        _gmlp_kernel,
        grid=(n // tm,),
        in_specs=[
            pl.BlockSpec((tm, d), row),
            pl.BlockSpec((1,) + mods.shape[1:], lambda i: (i // tiles_per_batch, 0, 0)),
        ] + [_const_spec(t.shape) for t in consts],
        out_specs=pl.BlockSpec((tm, d), row),
        out_shape=jax.ShapeDtypeStruct((n, d), F32),
        scratch_shapes=[pltpu.VMEM((tm, width), F32)],
        compiler_params=_params("parallel"),
        name="gmlp_mixer",
    )(h2d, mods, *consts)


def kernel(x, c, ctx, c_ctx, mod_w, mod_b, ln_g, ln_b, gla_w_in, gla_w_decay, gla_b_decay, gla_norm_g,
           gla_w_out, gm_w_in, gm_ln_g, gm_ln_b, gm_w_s, gm_b_s, gm_w_out, ffn_w_gate, ffn_w_up, ffn_w_down):
    bsz, l, d = x.shape
    lc = ctx.shape[1]
    n = bsz * l
    assert bsz + 1 <= COND_ROWS
    dk = gla_w_decay.shape[-1]
    dv = gla_w_out.shape[1]
    q_scale = (dk // GLA_HEADS) ** -0.5
    vec = lambda t: t.reshape(1, -1)

    cond = jnp.concatenate([c, c_ctx[None], jnp.zeros((COND_ROWS - bsz - 1, d), F32)], axis=0)
    mods, w_lat, w_ctx, wdec = _adaln(cond, mod_w, mod_b, gla_w_in[0], gla_w_decay[0], dk, dv)
    mods = mods.reshape(DEPTH, COND_ROWS, 6, d)

    bdec = gla_b_decay[0].reshape(2, 1, dk)

    tpb = l // PROJ_TILE
    later = (gla_w_out, gm_w_in, gm_w_out, ffn_w_gate, ffn_w_up, ffn_w_down)
    ((v, gate, qin_f, kin_f, kst_f, dec_f, qin_b, kin_b, kst_b, dec_b),
     (w_out, gm_in, gm_out, wg0, wg1, wu0, wu1, wd0, wd1)) = _gla_proj(
        x.reshape(n, d), mods[0], w_lat, wdec, bdec, lambda i: i // tpb, True, PROJ_TILE, dk, dv, q_scale,
        cast=later)
    (vc, kstc_f, decc_f, kstc_b, decc_b), _ = _gla_proj(
        ctx.reshape(bsz * lc, d), mods[0], w_ctx, wdec, bdec, lambda i: bsz, False, lc, dk, dv, q_scale)
    b3 = lambda t: t.reshape(bsz, l, -1)
    v3, vc3 = b3(v), vc.reshape(bsz, lc, dv)
    o_b = _gla_scan(b3(qin_b), b3(kin_b), kst_b, v3, dec_b, kstc_b, vc3, decc_b, True)
    o = _gla_scan(b3(qin_f), b3(kin_f), kst_f, v3, dec_f, kstc_f, vc3, decc_f, False, prev=o_b)

    ffn_w = ((wg0, wu0, wd0), (wg1, wu1, wd1))
    ffn_consts = lambda i: ffn_w[i] + (vec(ln_g[i, 1]), vec(ln_b[i, 1]))
    gla_consts = (vec(gla_norm_g[0]), w_out, vec(ln_g[0, 0]), vec(ln_b[0, 0]))
    tpf = l // FFN_TILE
    h = _ffn((o.reshape(n, dv), gate, x.reshape(n, d)), mods[0], gla_consts + ffn_consts(0), tpf, True)

    bs = jnp.broadcast_to(gm_b_s[0].T[:, :, None], (GM_GROUPS, GM_CHUNK, LANES))
    gm_consts = (gm_in, vec(gm_ln_g[0]), vec(gm_ln_b[0]), gm_w_s[0].astype(BF16), bs,
                 gm_out, vec(ln_g[1, 0]), vec(ln_b[1, 0]))
    h = _gmlp(h, mods[1], gm_consts, l // GMLP_TILE)
    h = _ffn((h,), mods[1], ffn_consts(1), tpf, False)
    return h.reshape(bsz, l, d)
```

```python
import functools

import jax
import jax.numpy as jnp
from jax import lax
from jax.experimental import pallas as pl
from jax.experimental.pallas import tpu as pltpu

F32 = jnp.float32
BF16 = jnp.bfloat16

DEPTH = 2
ALPHA = (2 * DEPTH) ** 0.25
GLA_HEADS = 4
GLA_RANK = 16
GLA_GATE_NORM = 16.0
GLA_CHUNK = 64
GM_GROUPS = 4
GM_CHUNK = 128
COND_ROWS = 8
LANES = 128
VMEM_LIMIT = 56 * 1024 * 1024

PROJ_TILE = 512
SCAN_BLOCK = 1024
DECAY_BLOCK = 256
KST_GROUP = 2
FFN_TILE = 1024
FFN_SUB = 512
FFN_HEAD_AT = (3, 6)
FFN_TAIL_PIECES = 8
FFN_COLS = 256
GMLP_TILE = 1024
GMLP_SUB = 512
GMLP_COLS = 512
GMLP_PIECES = (96, 96, 80, 80, 80, 80)


def _dot(a, b):
    return jnp.dot(a, b, preferred_element_type=F32)


def _layer_norm(x, g, b, eps=1e-5):
    mu = jnp.mean(x, axis=-1, keepdims=True)
    xc = x - mu
    var = jnp.mean(xc * xc, axis=-1, keepdims=True)
    return xc * lax.rsqrt(var + eps) * g + b


def _silu(x):
    return x * jax.nn.sigmoid(x)


def _lane_fold(t):
    acc = t[:, 0:LANES]
    for j in range(1, t.shape[1] // LANES):
        acc = acc + t[:, j * LANES:(j + 1) * LANES]
    return acc


def _order_after(x, dep):
    bits = pltpu.bitcast(dep[0:x.shape[0], 0:LANES], jnp.uint32)
    half = jnp.uint32(16)
    zero = pltpu.bitcast(lax.shift_right_logical(lax.shift_right_logical(bits, half), half), F32)
    return jnp.concatenate([x[:, 0:LANES] + zero, x[:, LANES:]], axis=1)


def _params(*sem):
    return pltpu.CompilerParams(dimension_semantics=sem, vmem_limit_bytes=VMEM_LIMIT)


def _const_spec(shape):
    nd = len(shape)
    return pl.BlockSpec(shape, lambda *_: (0,) * nd, pipeline_mode=pl.Buffered(1))


def _adaln_kernel(cond_ref, w_ref, b_ref, o_ref):
    s = _silu(cond_ref[...]).astype(BF16)
    o_ref[0] = _dot(s, w_ref[0].astype(BF16)) + b_ref[0]


def _adaln(cond, mod_w, mod_b):
    depth, d, n = mod_w.shape
    tn = n // 4
    return pl.pallas_call(
        _adaln_kernel,
        grid=(depth, n // tn),
        in_specs=[
            pl.BlockSpec((COND_ROWS, d), lambda i, j: (0, 0)),
            pl.BlockSpec((1, d, tn), lambda i, j: (i, 0, j)),
            pl.BlockSpec((1, 1, tn), lambda i, j: (i, 0, j)),
        ],
        out_specs=pl.BlockSpec((1, COND_ROWS, tn), lambda i, j: (i, 0, j)),
        out_shape=jax.ShapeDtypeStruct((depth, COND_ROWS, n), F32),
        compiler_params=_params("arbitrary", "arbitrary"),
        name="adaln",
    )(cond, mod_w, mod_b.reshape(depth, 1, n))


def _log_sigmoid(z):
    return jnp.minimum(z, 0.0) - jnp.log(1.0 + jnp.exp(-jnp.abs(z)))


def _chunk_tri(n, reverse):
    row = lax.broadcasted_iota(jnp.int32, (n, n), 0)
    col = lax.broadcasted_iota(jnp.int32, (n, n), 1)
    shift = GLA_CHUNK.bit_length() - 1
    same = jnp.right_shift(row, shift) == jnp.right_shift(col, shift)
    return jnp.where(same & ((col >= row) if reverse else (col <= row)), 1.0, 0.0).astype(BF16)


def _gla_proj_kernel(x_ref, mod_ref, w_ref, wdec_ref, bdec_ref, *refs, latent, dk, dv, q_scale, cast_layers):
    n_src, n_dst = len(cast_layers), sum(cast_layers)
    dsts = iter(refs[len(refs) - n_dst:])
    for src, layers in zip(refs[:n_src], cast_layers):
        for layer in range(layers):
            next(dsts)[0] = src[layer, 0].astype(BF16)
    out_refs = refs[n_src:len(refs) - n_dst]
    m = mod_ref[0]
    a = (x_ref[...] * (1.0 + m[1:2]) + m[0:1]).astype(BF16)
    tm = a.shape[0]
    nblk = tm // DECAY_BLOCK
    ncb = DECAY_BLOCK // GLA_CHUNK
    o_a = 2 * dk if latent else dk
    o_v = o_a + LANES
    if latent:
        v_ref, gate_ref = out_refs[:2]
        dir_refs = (out_refs[2:6], out_refs[6:10])
    else:
        v_ref = out_refs[0]
        dir_refs = ((None, None) + tuple(out_refs[1:3]), (None, None) + tuple(out_refs[3:5]))
    k = _dot(a, w_ref[:, 0:dk])
    if latent:
        q = _dot(a, w_ref[:, dk:2 * dk]) * q_scale
    a_lr = _dot(a, w_ref[:, o_a:o_a + LANES]).astype(BF16)
    z = [_dot(a_lr, wdec_ref[rev]) + bdec_ref[rev] for rev in (0, 1)]
    v_ref[...] = _dot(a, w_ref[:, o_v:o_v + dv]).astype(BF16)
    g_parts = []
    for rev in (0, 1):
        g = _log_sigmoid(z[rev]) * (1.0 / GLA_GATE_NORM)
        g_hi = g.astype(BF16)
        g_parts.append((g_hi, (g - g_hi.astype(F32)).astype(BF16)))
    tris = (_chunk_tri(DECAY_BLOCK, False), _chunk_tri(DECAY_BLOCK, True))
    b = {}
    for rev in (0, 1):
        for blk in range(nblk):
            rows = slice(blk * DECAY_BLOCK, (blk + 1) * DECAY_BLOCK)
            b[rev, blk] = _dot(tris[rev], g_parts[rev][0][rows]) + _dot(tris[rev], g_parts[rev][1][rows])
    if latent:
        gate_ref[...] = _silu(_dot(a, w_ref[:, o_v + dv:o_v + 2 * dv])).astype(BF16)
    for rev in (0, 1):
        qin_ref, kin_ref, kst_ref, dec_ref = dir_refs[rev]
        group_chunks = dec_ref.shape[1]
        for blk in range(nblk):
            rows = slice(blk * DECAY_BLOCK, (blk + 1) * DECAY_BLOCK)
            bb = b[rev, blk]
            if latent:
                qin_ref[rows, :] = (q[rows] * jnp.exp(bb)).astype(BF16)
                kin_ref[rows, :] = (k[rows] * jnp.exp(-bb)).astype(BF16)
            ends, kst = [], []
            for c in range(ncb):
                lo = c * GLA_CHUNK
                b_c = bb[lo:lo + GLA_CHUNK]
                b_end = b_c[0:1] if rev else b_c[GLA_CHUNK - 1:GLA_CHUNK]
                kst.append(k[blk * DECAY_BLOCK + lo:blk * DECAY_BLOCK + lo + GLA_CHUNK] * jnp.exp(b_end - b_c))
                ends.append(b_end)
            for p in range(ncb // KST_GROUP):
                grp = jnp.concatenate(kst[p * KST_GROUP:(p + 1) * KST_GROUP], axis=0)
                kst_ref[blk * (ncb // KST_GROUP) + p] = grp.T.astype(BF16)
            first = blk * ncb
            dec_ref[first // group_chunks, first % group_chunks:first % group_chunks + ncb, :] = jnp.exp(
                jnp.concatenate(ends, axis=0))


def _gla_proj(x2d, mods, w, wdec, bdec, mod_row, latent, tm, dk, dv, q_scale, cast=()):
    n, d = x2d.shape
    steps = n // tm
    group = min(SCAN_BLOCK, tm)
    row = lambda i: (i, 0)
    row3 = lambda i: (i, 0, 0)
    slabs = [t.reshape(t.shape[0], steps, t.shape[1] // steps, t.shape[2]) for t in cast]
    slab_in = [pl.BlockSpec((t.shape[0], 1) + t.shape[2:], lambda i: (0, i, 0, 0)) for t in slabs]
    slab_out = [(jax.ShapeDtypeStruct(t.shape[1:], BF16), pl.BlockSpec((1,) + t.shape[2:], row3))
                for t in slabs for _ in range(t.shape[0])]
    tok = lambda c: (jax.ShapeDtypeStruct((n, c), BF16), pl.BlockSpec((tm, c), row))
    kst = (jax.ShapeDtypeStruct((n // LANES, dk, LANES), BF16),
           pl.BlockSpec((tm // LANES, dk, LANES), row3))
    dec = (jax.ShapeDtypeStruct((n // group, group // GLA_CHUNK, dk), F32),
           pl.BlockSpec((tm // group, group // GLA_CHUNK, dk), row3))
    per_dir = [tok(dk), tok(dk), kst, dec] if latent else [kst, dec]
    outs = ([tok(dv), tok(dv)] if latent else [tok(dv)]) + per_dir + per_dir
    res = pl.pallas_call(
        functools.partial(_gla_proj_kernel, latent=latent, dk=dk, dv=dv, q_scale=q_scale,
                          cast_layers=tuple(t.shape[0] for t in cast)),
        grid=(steps,),
        in_specs=[
            pl.BlockSpec((tm, d), row),
            pl.BlockSpec((1,) + mods.shape[1:], lambda i: (mod_row(i), 0, 0)),
            _const_spec(w.shape), _const_spec(wdec.shape), _const_spec(bdec.shape),
        ] + slab_in,
        out_specs=[o[1] for o in outs] + [o[1] for o in slab_out],
        out_shape=[o[0] for o in outs] + [o[0] for o in slab_out],
        compiler_params=_params("parallel"),
        name="gla_in_proj" if latent else "gla_ctx_proj",
    )(x2d, mods, w, wdec, bdec, *slabs)
    n_out = len(outs)
    shapes = [t.shape[1:] for t in cast for _ in range(t.shape[0])]
    return res[:n_out], [r.reshape(s) for r, s in zip(res[n_out:], shapes)]


def _decay_columns(dec_row, dvh):
    dkh = dec_row.shape[1]
    dcol = jnp.broadcast_to(dec_row, (dkh, dkh)).T
    return jnp.concatenate([dcol] * (dvh // dkh), axis=1)


def _gla_scan_kernel(*refs, reverse, add_prev):
    qin_ref, kin_ref, kst_ref, v_ref, dec_ref, kstc_ref, vc_ref, decc_ref = refs[:8]
    if add_prev:
        prev_ref, o_ref, s_ref = refs[8:]
    else:
        o_ref, s_ref = refs[8:]
    dkh, dvh = s_ref.shape[1:]
    heads = range(GLA_HEADS)
    ks = [slice(h * dkh, (h + 1) * dkh) for h in heads]
    vs = [slice(h * dvh, (h + 1) * dvh) for h in heads]

    def order(nc):
        return list(range(nc - 1, -1, -1) if reverse else range(nc))

    def rows(c):
        return slice(c * GLA_CHUNK, (c + 1) * GLA_CHUNK)

    lane_chunk = lax.broadcasted_iota(jnp.int32, (dkh, LANES), 1) // GLA_CHUNK

    def state_update(kst_g_ref, v_g_ref, c, h):
        g = c // KST_GROUP
        keys = jnp.where(lane_chunk == c % KST_GROUP, kst_g_ref[g, ks[h], :], jnp.zeros((), BF16))
        return _dot(keys, v_g_ref[0, g * LANES:(g + 1) * LANES, vs[h]])

    @pl.when(pl.program_id(1) == 0)
    def _():
        chunks = order(vc_ref.shape[1] // GLA_CHUNK)
        upd = {(c, h): state_update(kstc_ref, vc_ref, c, h) for c in chunks for h in heads}
        for h in heads:
            s = jnp.zeros((dkh, dvh), F32)
            for c in chunks:
                s = s * _decay_columns(decc_ref[0, c:c + 1, ks[h]], dvh) + upd[c, h]
            s_ref[h] = s

    chunks = order(v_ref.shape[1] // GLA_CHUNK)
    pairs = [(c, h) for c in chunks for h in heads]
    r64 = lax.broadcasted_iota(jnp.int32, (GLA_CHUNK, GLA_CHUNK), 0)
    c64 = lax.broadcasted_iota(jnp.int32, (GLA_CHUNK, GLA_CHUNK), 1)
    mask = (c64 >= r64) if reverse else (c64 <= r64)
    att = {(c, h): lax.dot_general(qin_ref[0, rows(c), ks[h]], kin_ref[0, rows(c), ks[h]],
                                   (((1,), (1,)), ((), ())), preferred_element_type=F32) for c, h in pairs}
    upd = {(c, h): state_update(kst_ref, v_ref, c, h) for c, h in pairs}
    dcol = {(c, h): _decay_columns(dec_ref[0, c:c + 1, ks[h]], dvh) for c, h in pairs}
    lhs = {(c, h): jnp.concatenate([qin_ref[0, rows(c), ks[h]], jnp.where(mask, att[c, h], 0.0).astype(BF16)],
                                   axis=1) for c, h in pairs}
    state = [s_ref[h] for h in heads]
    for c in chunks:
        outs = []
        for h in heads:
            rhs = jnp.concatenate([state[h].astype(BF16), v_ref[0, rows(c), vs[h]]], axis=0)
            outs.append(_dot(lhs[c, h], rhs))
            state[h] = state[h] * dcol[c, h] + upd[c, h]
        o = jnp.concatenate(outs, axis=1)
        if add_prev:
            o = o + prev_ref[0, rows(c), :].astype(F32)
        o_ref[0, rows(c), :] = o.astype(o_ref.dtype)
    for h in heads:
        s_ref[h] = state[h]


def _gla_scan(qin, kin, kst, v, dec, kstc, vc, decc, reverse, prev=None):
    bsz, l, dk = qin.shape
    dv = v.shape[2]
    lc = vc.shape[1]
    tb = SCAN_BLOCK
    nblk = l // tb
    ncb = tb // GLA_CHUNK

    def pos(j):
        return (nblk - 1 - j) if reverse else j

    blk = lambda b, j: (b, pos(j), 0)
    flat = lambda b, j: (b * nblk + pos(j), 0, 0)
    ctx = lambda b, j: (b, 0, 0)
    in_specs = [
        pl.BlockSpec((1, tb, dk), blk), pl.BlockSpec((1, tb, dk), blk),
        pl.BlockSpec((tb // LANES, dk, LANES), flat), pl.BlockSpec((1, tb, dv), blk),
        pl.BlockSpec((1, ncb, dk), flat),
        pl.BlockSpec((lc // LANES, dk, LANES), ctx), pl.BlockSpec((1, lc, dv), ctx),
        pl.BlockSpec((1, lc // GLA_CHUNK, dk), ctx),
    ]
    args = [qin, kin, kst, v, dec.reshape(bsz * nblk, ncb, dk), kstc, vc, decc]
    if prev is not None:
        in_specs.append(pl.BlockSpec((1, tb, dv), blk))
        args.append(prev)
    return pl.pallas_call(
        functools.partial(_gla_scan_kernel, reverse=reverse, add_prev=prev is not None),
        grid=(bsz, nblk),
        in_specs=in_specs,
        out_specs=pl.BlockSpec((1, tb, dv), blk),
        out_shape=jax.ShapeDtypeStruct((bsz, l, dv), BF16),
        scratch_shapes=[pltpu.VMEM((GLA_HEADS, dk // GLA_HEADS, dv // GLA_HEADS), F32)],
        compiler_params=_params("arbitrary", "arbitrary"),
        name="gla_scan_bwd" if reverse else "gla_scan_fwd",
    )(*args)


def _ffn_kernel(*refs, gla_prologue):
    if gla_prologue:
        (o_ref, gate_ref, x_ref, mod_ref, ng_ref, wout_ref, lng0_ref, lnb0_ref,
         wg_ref, wu_ref, wd_ref, lng_ref, lnb_ref, out_ref, hid_ref) = refs
    else:
        h_ref, mod_ref, wg_ref, wu_ref, wd_ref, lng_ref, lnb_ref, out_ref, hid_ref = refs
    m = mod_ref[0]
    hidden = wg_ref.shape[1]
    nsub = out_ref.shape[0] // FFN_SUB

    def head(lo, size):
        rows = slice(lo, lo + size)
        if gla_prologue:
            o = o_ref[rows, :].astype(F32)
            dvh = o.shape[1] // GLA_HEADS
            normed = []
            for hd in range(GLA_HEADS):
                o_h = o[:, hd * dvh:(hd + 1) * dvh]
                ms = jnp.mean(o_h * o_h, axis=-1, keepdims=True)
                normed.append(o_h * lax.rsqrt(ms + 1e-6) * ng_ref[...])
            gated = (jnp.concatenate(normed, axis=1) * gate_ref[rows, :].astype(F32)).astype(BF16)
            y = _dot(gated, wout_ref[...])
            h = _layer_norm(ALPHA * x_ref[rows, :] + m[2:3] * y, lng0_ref[...], lnb0_ref[...])
        else:
            h = h_ref[rows, :]
        af = h * (1.0 + m[4:5]) + m[3:4]
        return h, af.astype(BF16), _lane_fold(af)

    def tail(lo, h, f, after=None):
        res = ALPHA * h + m[5:6] * f
        if after is not None:
            res = _order_after(res, after)
        out_ref[lo:lo + h.shape[0], :] = _layer_norm(res, lng_ref[...], lnb_ref[...])

    head_rows = FFN_SUB // len(FFN_HEAD_AT)
    tail_rows = FFN_SUB // FFN_TAIL_PIECES
    h, a, _ = head(0, FFN_SUB)
    pending = None
    for sub in range(nsub):
        nxt, done_tail = [], 0
        for j, lo in enumerate(range(0, hidden, FFN_COLS)):
            piece = None
            if sub + 1 < nsub and j in FFN_HEAD_AT:
                piece = head((sub + 1) * FFN_SUB + FFN_HEAD_AT.index(j) * head_rows, head_rows)
                nxt.append(piece)
            g = _dot(a, wg_ref[:, lo:lo + FFN_COLS])
            u = _dot(a, wu_ref[:, lo:lo + FFN_COLS])
            prod = _silu(g) * u
            if piece is not None:
                prod = jnp.concatenate([_order_after(prod[0:head_rows], piece[2]), prod[head_rows:]], axis=0)
            hid_ref[sub, :, lo:lo + FFN_COLS] = prod.astype(BF16)
            if pending is not None and piece is None and done_tail < FFN_TAIL_PIECES:
                p_lo, p_h, p_f = pending
                r = slice(done_tail * tail_rows, (done_tail + 1) * tail_rows)
                tail(p_lo + done_tail * tail_rows, p_h[r], p_f[r], after=g)
                done_tail += 1
        assert pending is None or done_tail == FFN_TAIL_PIECES
        pending = (sub * FFN_SUB, h, _dot(hid_ref[sub], wd_ref[...]))
        if nxt:
            h = jnp.concatenate([p[0] for p in nxt], axis=0)
            a = jnp.concatenate([p[1] for p in nxt], axis=0)
    tail(*pending)


def _ffn(acts, mods, consts, tiles_per_batch, gla_prologue):
    n, d = acts[0].shape
    tm = FFN_TILE
    hidden = consts[-5].shape[1]
    row = lambda i: (i, 0)
    mod_spec = pl.BlockSpec((1,) + mods.shape[1:], lambda i: (i // tiles_per_batch, 0, 0))
    in_specs = ([pl.BlockSpec((tm, t.shape[1]), row) for t in acts] + [mod_spec]
                + [_const_spec(t.shape) for t in consts])
    return pl.pallas_call(
        functools.partial(_ffn_kernel, gla_prologue=gla_prologue),
        grid=(n // tm,),
        in_specs=in_specs,
        out_specs=pl.BlockSpec((tm, d), row),
        out_shape=jax.ShapeDtypeStruct((n, d), F32),
        scratch_shapes=[pltpu.VMEM((tm // FFN_SUB, FFN_SUB, hidden), BF16)],
        compiler_params=_params("parallel"),
        name="gla_out_ffn" if gla_prologue else "swiglu_ffn",
    )(*acts, mods, *consts)


def _gmlp_kernel(h_ref, mod_ref, win_ref, vg_ref, vb_ref, ws_ref, bs_ref, wout_ref, lng_ref, lnb_ref,
                 o_ref, z_scr, vn_scr):
    m = mod_ref[0]
    width = z_scr.shape[2]
    gw = width // GM_GROUPS
    nsub = o_ref.shape[0] // GMLP_SUB
    chunks = list(range(0, width, GMLP_COLS))
    starts = [sum(GMLP_PIECES[:i]) for i in range(len(GMLP_PIECES))]
    assert sum(GMLP_PIECES) == GMLP_SUB and len(GMLP_PIECES) == len(chunks) and GM_GROUPS >= 3

    def gelu(z):
        return 0.5 * z * (1.0 + lax.erf(z * (0.5 ** 0.5)))

    def head(sub):
        h = h_ref[sub * GMLP_SUB:(sub + 1) * GMLP_SUB, :]
        return h, (h * (1.0 + m[1:2]) + m[0:1]).astype(BF16)

    def v_project(sub, a, between):
        for j, lo in enumerate(chunks):
            raw = _dot(a, win_ref[:, width + lo:width + lo + GMLP_COLS])
            z_scr[sub, :, lo:lo + GMLP_COLS] = raw
            between(j, raw)

    def normalise_piece(sub, p, after):
        r = slice(starts[p], starts[p] + GMLP_PIECES[p])
        v = gelu(_order_after(z_scr[sub, r, :], after))
        vn_scr[sub, r, :] = _layer_norm(v, vg_ref[...], vb_ref[...]).astype(BF16)

    def mix(sub, a, between):
        event = 0
        u_raw = _dot(a, win_ref[:, 0:gw])
        between(event, u_raw)
        y = None
        for gi in range(GM_GROUPS):
            cols = slice(gi * gw, (gi + 1) * gw)
            u = gelu(u_raw)
            if gi + 1 < GM_GROUPS:
                u_raw = _dot(a, win_ref[:, (gi + 1) * gw:(gi + 2) * gw])
                event += 1
                between(event, u_raw)
            bias = jnp.concatenate([bs_ref[gi]] * (gw // LANES), axis=1)
            gated = []
            for t in range(GMLP_SUB // GM_CHUNK):
                rows = slice(t * GM_CHUNK, (t + 1) * GM_CHUNK)
                s = _dot(ws_ref[gi], vn_scr[sub, rows, cols]) + bias
                gated.append((u[rows] * s).astype(BF16))
            y_g = _dot(jnp.concatenate(gated, axis=0), wout_ref[cols, :])
            y = y_g if y is None else y + y_g
            if gi + 2 < GM_GROUPS:
                event += 1
                between(event, y_g)
        assert event == len(GMLP_PIECES) - 1
        return y

    def tail_piece(sub, h, y, p, after=None):
        r = slice(starts[p], starts[p] + GMLP_PIECES[p])
        res = ALPHA * h[r] + m[2:3] * y[r]
        if after is not None:
            res = _order_after(res, after)
        lo = sub * GMLP_SUB + starts[p]
        o_ref[lo:lo + GMLP_PIECES[p], :] = _layer_norm(res, lng_ref[...], lnb_ref[...])

    assert nsub == 2
    (h0, a0), (h1, a1) = head(0), head(1)
    v_project(0, a0, lambda j, raw: None)
    v_project(1, a1, lambda j, raw: normalise_piece(0, j, raw))
    y0 = mix(0, a0, lambda e, result: normalise_piece(1, e, result))
    y1 = mix(1, a1, lambda e, result: tail_piece(0, h0, y0, e, result))
    for p in range(len(GMLP_PIECES)):
        tail_piece(1, h1, y1, p)


def _gmlp(h2d, mods, consts, tiles_per_batch):
    n, d = h2d.shape
    tm = GMLP_TILE
    width = consts[5].shape[0]
    row = lambda i: (i, 0)
    return pl.pallas_call(
        _gmlp_kernel,
        grid=(n // tm,),
        in_specs=[
            pl.BlockSpec((tm, d), row),
            pl.BlockSpec((1,) + mods.shape[1:], lambda i: (i // tiles_per_batch, 0, 0)),
        ] + [_const_spec(t.shape) for t in consts],
        out_specs=pl.BlockSpec((tm, d), row),
        out_shape=jax.ShapeDtypeStruct((n, d), F32),
        scratch_shapes=[pltpu.VMEM((tm // GMLP_SUB, GMLP_SUB, width), F32),
                        pltpu.VMEM((tm // GMLP_SUB, GMLP_SUB, width), BF16)],
        compiler_params=_params("parallel"),
        name="gmlp_mixer",
    )(h2d, mods, *consts)


def kernel(x, c, ctx, c_ctx, mod_w, mod_b, ln_g, ln_b, gla_w_in, gla_w_decay, gla_b_decay, gla_norm_g,
           gla_w_out, gm_w_in, gm_ln_g, gm_ln_b, gm_w_s, gm_b_s, gm_w_out, ffn_w_gate, ffn_w_up, ffn_w_down):
    bsz, l, d = x.shape
    lc = ctx.shape[1]
    n = bsz * l
    assert bsz + 1 <= COND_ROWS
    dk = gla_w_decay.shape[-1]
    dv = gla_w_out.shape[1]
    q_scale = (dk // GLA_HEADS) ** -0.5
    vec = lambda t: t.reshape(1, -1)

    cond = jnp.concatenate([c, c_ctx[None], jnp.zeros((COND_ROWS - bsz - 1, d), F32)], axis=0)
    mods = _adaln(cond, mod_w, mod_b).reshape(DEPTH, COND_ROWS, 6, d)

    w_in = gla_w_in[0]
    o_a = dk + dv
    o_q = o_a + 2 * GLA_RANK
    w_k, w_v, w_a = w_in[:, :dk], w_in[:, dk:o_a], w_in[:, o_a:o_q]
    w_q, w_r = w_in[:, o_q:o_q + dk], w_in[:, o_q + dk:]
    w_a = jnp.pad(w_a, ((0, 0), (0, LANES - 2 * GLA_RANK)))
    w_lat = jnp.concatenate([w_k, w_q, w_a, w_v, w_r], axis=1).astype(BF16)
    w_ctx = jnp.concatenate([w_k, w_a, w_v], axis=1).astype(BF16)
    wdec = jnp.zeros((2, LANES, dk), F32)
    wdec = wdec.at[0, :GLA_RANK].set(gla_w_decay[0, 0]).at[1, GLA_RANK:2 * GLA_RANK].set(gla_w_decay[0, 1])
    wdec = wdec.astype(BF16)
    bdec = gla_b_decay[0].reshape(2, 1, dk)

    tpb = l // PROJ_TILE
    later = (gla_w_out, gm_w_in, gm_w_out, ffn_w_gate, ffn_w_up, ffn_w_down)
    ((v, gate, qin_f, kin_f, kst_f, dec_f, qin_b, kin_b, kst_b, dec_b),
     (w_out, gm_in, gm_out, wg0, wg1, wu0, wu1, wd0, wd1)) = _gla_proj(
        x.reshape(n, d), mods[0], w_lat, wdec, bdec, lambda i: i // tpb, True, PROJ_TILE, dk, dv, q_scale,
        cast=later)
    (vc, kstc_f, decc_f, kstc_b, decc_b), _ = _gla_proj(
        ctx.reshape(bsz * lc, d), mods[0], w_ctx, wdec, bdec, lambda i: bsz, False, lc, dk, dv, q_scale)
    b3 = lambda t: t.reshape(bsz, l, -1)
    v3, vc3 = b3(v), vc.reshape(bsz, lc, dv)
    o_b = _gla_scan(b3(qin_b), b3(kin_b), kst_b, v3, dec_b, kstc_b, vc3, decc_b, True)
    o = _gla_scan(b3(qin_f), b3(kin_f), kst_f, v3, dec_f, kstc_f, vc3, decc_f, False, prev=o_b)

    ffn_w = ((wg0, wu0, wd0), (wg1, wu1, wd1))
    ffn_consts = lambda i: ffn_w[i] + (vec(ln_g[i, 1]), vec(ln_b[i, 1]))
    gla_consts = (vec(gla_norm_g[0]), w_out, vec(ln_g[0, 0]), vec(ln_b[0, 0]))
    tpf = l // FFN_TILE
    h = _ffn((o.reshape(n, dv), gate, x.reshape(n, d)), mods[0], gla_consts + ffn_consts(0), tpf, True)

    bs = jnp.broadcast_to(gm_b_s[0].T[:, :, None], (GM_GROUPS, GM_CHUNK, LANES))
    gm_consts = (gm_in, vec(gm_ln_g[0]), vec(gm_ln_b[0]), gm_w_s[0].astype(BF16), bs,
                 gm_out, vec(ln_g[1, 0]), vec(ln_b[1, 0]))
    h = _gmlp(h, mods[1], gm_consts, l // GMLP_TILE)
    h = _ffn((h,), mods[1], ffn_consts(1), tpf, False)
    return h.reshape(bsz, l, d)
```

```python
import functools

import jax
import jax.numpy as jnp
from jax import lax
from jax.experimental import pallas as pl
from jax.experimental.pallas import tpu as pltpu

F32 = jnp.float32
BF16 = jnp.bfloat16

DEPTH = 2
ALPHA = (2 * DEPTH) ** 0.25
GLA_HEADS = 4
GLA_RANK = 16
GLA_GATE_NORM = 16.0
GLA_CHUNK = 64
GM_GROUPS = 4
GM_CHUNK = 128
COND_ROWS = 8
LANES = 128
VMEM_LIMIT = 56 * 1024 * 1024

PROJ_TILE = 512
SCAN_BLOCK = 1024
DECAY_BLOCK = 256
KST_GROUP = 2
FFN_TILE = 1024
FFN_SUB = 512
FFN_HEAD_AT = (3, 6)
FFN_TAIL_PIECES = 8
FFN_COLS = 256
GMLP_TILE = 512
GMLP_SUB = 256
GMLP_COLS = 512
GMLP_PIECES = (48, 48, 48, 48, 32, 32)


def _dot(a, b):
    return jnp.dot(a, b, preferred_element_type=F32)


def _layer_norm(x, g, b, eps=1e-5):
    mu = jnp.mean(x, axis=-1, keepdims=True)
    xc = x - mu
    var = jnp.mean(xc * xc, axis=-1, keepdims=True)
    return xc * lax.rsqrt(var + eps) * g + b


def _silu(x):
    return x * jax.nn.sigmoid(x)


def _lane_fold(t):
    acc = t[:, 0:LANES]
    for j in range(1, t.shape[1] // LANES):
        acc = acc + t[:, j * LANES:(j + 1) * LANES]
    return acc


def _order_after(x, dep):
    bits = pltpu.bitcast(dep[0:x.shape[0], 0:LANES], jnp.uint32)
    half = jnp.uint32(16)
    zero = pltpu.bitcast(lax.shift_right_logical(lax.shift_right_logical(bits, half), half), F32)
    return jnp.concatenate([x[:, 0:LANES] + zero, x[:, LANES:]], axis=1)


def _params(*sem):
    return pltpu.CompilerParams(dimension_semantics=sem, vmem_limit_bytes=VMEM_LIMIT)


def _const_spec(shape):
    nd = len(shape)
    return pl.BlockSpec(shape, lambda *_: (0,) * nd, pipeline_mode=pl.Buffered(1))


def _adaln_kernel(cond_ref, w_ref, b_ref, o_ref):
    s = _silu(cond_ref[...]).astype(BF16)
    o_ref[0] = _dot(s, w_ref[0].astype(BF16)) + b_ref[0]


def _adaln(cond, mod_w, mod_b):
    depth, d, n = mod_w.shape
    tn = n // 4
    return pl.pallas_call(
        _adaln_kernel,
        grid=(depth, n // tn),
        in_specs=[
            pl.BlockSpec((COND_ROWS, d), lambda i, j: (0, 0)),
            pl.BlockSpec((1, d, tn), lambda i, j: (i, 0, j)),
            pl.BlockSpec((1, 1, tn), lambda i, j: (i, 0, j)),
        ],
        out_specs=pl.BlockSpec((1, COND_ROWS, tn), lambda i, j: (i, 0, j)),
        out_shape=jax.ShapeDtypeStruct((depth, COND_ROWS, n), F32),
        compiler_params=_params("arbitrary", "arbitrary"),
        name="adaln",
    )(cond, mod_w, mod_b.reshape(depth, 1, n))


def _log_sigmoid(z):
    return jnp.minimum(z, 0.0) - jnp.log(1.0 + jnp.exp(-jnp.abs(z)))


def _chunk_tri(n, reverse):
    row = lax.broadcasted_iota(jnp.int32, (n, n), 0)
    col = lax.broadcasted_iota(jnp.int32, (n, n), 1)
    shift = GLA_CHUNK.bit_length() - 1
    same = jnp.right_shift(row, shift) == jnp.right_shift(col, shift)
    return jnp.where(same & ((col >= row) if reverse else (col <= row)), 1.0, 0.0).astype(BF16)


def _gla_proj_kernel(x_ref, mod_ref, w_ref, wdec_ref, bdec_ref, *refs, latent, dk, dv, q_scale, cast_layers):
    n_src, n_dst = len(cast_layers), sum(cast_layers)
    dsts = iter(refs[len(refs) - n_dst:])
    for src, layers in zip(refs[:n_src], cast_layers):
        for layer in range(layers):
            next(dsts)[0] = src[layer, 0].astype(BF16)
    out_refs = refs[n_src:len(refs) - n_dst]
    m = mod_ref[0]
    a = (x_ref[...] * (1.0 + m[1:2]) + m[0:1]).astype(BF16)
    tm = a.shape[0]
    nblk = tm // DECAY_BLOCK
    ncb = DECAY_BLOCK // GLA_CHUNK
    o_a = 2 * dk if latent else dk
    o_v = o_a + LANES
    if latent:
        v_ref, gate_ref = out_refs[:2]
        dir_refs = (out_refs[2:6], out_refs[6:10])
    else:
        v_ref = out_refs[0]
        dir_refs = ((None, None) + tuple(out_refs[1:3]), (None, None) + tuple(out_refs[3:5]))
    k = _dot(a, w_ref[:, 0:dk])
    if latent:
        q = _dot(a, w_ref[:, dk:2 * dk]) * q_scale
    a_lr = _dot(a, w_ref[:, o_a:o_a + LANES]).astype(BF16)
    z = [_dot(a_lr, wdec_ref[rev]) + bdec_ref[rev] for rev in (0, 1)]
    v_ref[...] = _dot(a, w_ref[:, o_v:o_v + dv]).astype(BF16)
    g_parts = []
    for rev in (0, 1):
        g = _log_sigmoid(z[rev]) * (1.0 / GLA_GATE_NORM)
        g_hi = g.astype(BF16)
        g_parts.append((g_hi, (g - g_hi.astype(F32)).astype(BF16)))
    tris = (_chunk_tri(DECAY_BLOCK, False), _chunk_tri(DECAY_BLOCK, True))
    b = {}
    for rev in (0, 1):
        for blk in range(nblk):
            rows = slice(blk * DECAY_BLOCK, (blk + 1) * DECAY_BLOCK)
            b[rev, blk] = _dot(tris[rev], g_parts[rev][0][rows]) + _dot(tris[rev], g_parts[rev][1][rows])
    if latent:
        gate_ref[...] = _silu(_dot(a, w_ref[:, o_v + dv:o_v + 2 * dv])).astype(BF16)
    for rev in (0, 1):
        qin_ref, kin_ref, kst_ref, dec_ref = dir_refs[rev]
        group_chunks = dec_ref.shape[1]
        for blk in range(nblk):
            rows = slice(blk * DECAY_BLOCK, (blk + 1) * DECAY_BLOCK)
            bb = b[rev, blk]
            if latent:
                qin_ref[rows, :] = (q[rows] * jnp.exp(bb)).astype(BF16)
                kin_ref[rows, :] = (k[rows] * jnp.exp(-bb)).astype(BF16)
            ends, kst = [], []
            for c in range(ncb):
                lo = c * GLA_CHUNK
                b_c = bb[lo:lo + GLA_CHUNK]
                b_end = b_c[0:1] if rev else b_c[GLA_CHUNK - 1:GLA_CHUNK]
                kst.append(k[blk * DECAY_BLOCK + lo:blk * DECAY_BLOCK + lo + GLA_CHUNK] * jnp.exp(b_end - b_c))
                ends.append(b_end)
            for p in range(ncb // KST_GROUP):
                grp = jnp.concatenate(kst[p * KST_GROUP:(p + 1) * KST_GROUP], axis=0)
                kst_ref[blk * (ncb // KST_GROUP) + p] = grp.T.astype(BF16)
            first = blk * ncb
            dec_ref[first // group_chunks, first % group_chunks:first % group_chunks + ncb, :] = jnp.exp(
                jnp.concatenate(ends, axis=0))


def _gla_proj(x2d, mods, w, wdec, bdec, mod_row, latent, tm, dk, dv, q_scale, cast=()):
    n, d = x2d.shape
    steps = n // tm
    group = min(SCAN_BLOCK, tm)
    row = lambda i: (i, 0)
    row3 = lambda i: (i, 0, 0)
    slabs = [t.reshape(t.shape[0], steps, t.shape[1] // steps, t.shape[2]) for t in cast]
    slab_in = [pl.BlockSpec((t.shape[0], 1) + t.shape[2:], lambda i: (0, i, 0, 0)) for t in slabs]
    slab_out = [(jax.ShapeDtypeStruct(t.shape[1:], BF16), pl.BlockSpec((1,) + t.shape[2:], row3))
                for t in slabs for _ in range(t.shape[0])]
    tok = lambda c: (jax.ShapeDtypeStruct((n, c), BF16), pl.BlockSpec((tm, c), row))
    kst = (jax.ShapeDtypeStruct((n // LANES, dk, LANES), BF16),
           pl.BlockSpec((tm // LANES, dk, LANES), row3))
    dec = (jax.ShapeDtypeStruct((n // group, group // GLA_CHUNK, dk), F32),
           pl.BlockSpec((tm // group, group // GLA_CHUNK, dk), row3))
    per_dir = [tok(dk), tok(dk), kst, dec] if latent else [kst, dec]
    outs = ([tok(dv), tok(dv)] if latent else [tok(dv)]) + per_dir + per_dir
    res = pl.pallas_call(
        functools.partial(_gla_proj_kernel, latent=latent, dk=dk, dv=dv, q_scale=q_scale,
                          cast_layers=tuple(t.shape[0] for t in cast)),
        grid=(steps,),
        in_specs=[
            pl.BlockSpec((tm, d), row),
            pl.BlockSpec((1,) + mods.shape[1:], lambda i: (mod_row(i), 0, 0)),
            _const_spec(w.shape), _const_spec(wdec.shape), _const_spec(bdec.shape),
        ] + slab_in,
        out_specs=[o[1] for o in outs] + [o[1] for o in slab_out],
        out_shape=[o[0] for o in outs] + [o[0] for o in slab_out],
        compiler_params=_params("parallel"),
        name="gla_in_proj" if latent else "gla_ctx_proj",
    )(x2d, mods, w, wdec, bdec, *slabs)
    n_out = len(outs)
    shapes = [t.shape[1:] for t in cast for _ in range(t.shape[0])]
    return res[:n_out], [r.reshape(s) for r, s in zip(res[n_out:], shapes)]


def _decay_columns(dec_row, dvh):
    dkh = dec_row.shape[1]
    dcol = jnp.broadcast_to(dec_row, (dkh, dkh)).T
    return jnp.concatenate([dcol] * (dvh // dkh), axis=1)


def _gla_scan_kernel(*refs, reverse, add_prev):
    qin_ref, kin_ref, kst_ref, v_ref, dec_ref, kstc_ref, vc_ref, decc_ref = refs[:8]
    if add_prev:
        prev_ref, o_ref, s_ref = refs[8:]
    else:
        o_ref, s_ref = refs[8:]
    dkh, dvh = s_ref.shape[1:]
    heads = range(GLA_HEADS)
    ks = [slice(h * dkh, (h + 1) * dkh) for h in heads]
    vs = [slice(h * dvh, (h + 1) * dvh) for h in heads]

    def order(nc):
        return list(range(nc - 1, -1, -1) if reverse else range(nc))

    def rows(c):
        return slice(c * GLA_CHUNK, (c + 1) * GLA_CHUNK)

    lane_chunk = lax.broadcasted_iota(jnp.int32, (dkh, LANES), 1) // GLA_CHUNK

    def state_update(kst_g_ref, v_g_ref, c, h):
        g = c // KST_GROUP
        keys = jnp.where(lane_chunk == c % KST_GROUP, kst_g_ref[g, ks[h], :], jnp.zeros((), BF16))
        return _dot(keys, v_g_ref[0, g * LANES:(g + 1) * LANES, vs[h]])

    @pl.when(pl.program_id(1) == 0)
    def _():
        chunks = order(vc_ref.shape[1] // GLA_CHUNK)
        upd = {(c, h): state_update(kstc_ref, vc_ref, c, h) for c in chunks for h in heads}
        for h in heads:
            s = jnp.zeros((dkh, dvh), F32)
            for c in chunks:
                s = s * _decay_columns(decc_ref[0, c:c + 1, ks[h]], dvh) + upd[c, h]
            s_ref[h] = s

    chunks = order(v_ref.shape[1] // GLA_CHUNK)
    pairs = [(c, h) for c in chunks for h in heads]
    r64 = lax.broadcasted_iota(jnp.int32, (GLA_CHUNK, GLA_CHUNK), 0)
    c64 = lax.broadcasted_iota(jnp.int32, (GLA_CHUNK, GLA_CHUNK), 1)
    mask = (c64 >= r64) if reverse else (c64 <= r64)
    att = {(c, h): lax.dot_general(qin_ref[0, rows(c), ks[h]], kin_ref[0, rows(c), ks[h]],
                                   (((1,), (1,)), ((), ())), preferred_element_type=F32) for c, h in pairs}
    upd = {(c, h): state_update(kst_ref, v_ref, c, h) for c, h in pairs}
    dcol = {(c, h): _decay_columns(dec_ref[0, c:c + 1, ks[h]], dvh) for c, h in pairs}
    lhs = {(c, h): jnp.concatenate([qin_ref[0, rows(c), ks[h]], jnp.where(mask, att[c, h], 0.0).astype(BF16)],
                                   axis=1) for c, h in pairs}
    state = [s_ref[h] for h in heads]
    for c in chunks:
        outs = []
        for h in heads:
            rhs = jnp.concatenate([state[h].astype(BF16), v_ref[0, rows(c), vs[h]]], axis=0)
            outs.append(_dot(lhs[c, h], rhs))
            state[h] = state[h] * dcol[c, h] + upd[c, h]
        o = jnp.concatenate(outs, axis=1)
        if add_prev:
            o = o + prev_ref[0, rows(c), :].astype(F32)
        o_ref[0, rows(c), :] = o.astype(o_ref.dtype)
    for h in heads:
        s_ref[h] = state[h]


def _gla_scan(qin, kin, kst, v, dec, kstc, vc, decc, reverse, prev=None):
    bsz, l, dk = qin.shape
    dv = v.shape[2]
    lc = vc.shape[1]
    tb = SCAN_BLOCK
    nblk = l // tb
    ncb = tb // GLA_CHUNK

    def pos(j):
        return (nblk - 1 - j) if reverse else j

    blk = lambda b, j: (b, pos(j), 0)
    flat = lambda b, j: (b * nblk + pos(j), 0, 0)
    ctx = lambda b, j: (b, 0, 0)
    in_specs = [
        pl.BlockSpec((1, tb, dk), blk), pl.BlockSpec((1, tb, dk), blk),
        pl.BlockSpec((tb // LANES, dk, LANES), flat), pl.BlockSpec((1, tb, dv), blk),
        pl.BlockSpec((1, ncb, dk), flat),
        pl.BlockSpec((lc // LANES, dk, LANES), ctx), pl.BlockSpec((1, lc, dv), ctx),
        pl.BlockSpec((1, lc // GLA_CHUNK, dk), ctx),
    ]
    args = [qin, kin, kst, v, dec.reshape(bsz * nblk, ncb, dk), kstc, vc, decc]
    if prev is not None:
        in_specs.append(pl.BlockSpec((1, tb, dv), blk))
        args.append(prev)
    return pl.pallas_call(
        functools.partial(_gla_scan_kernel, reverse=reverse, add_prev=prev is not None),
        grid=(bsz, nblk),
        in_specs=in_specs,
        out_specs=pl.BlockSpec((1, tb, dv), blk),
        out_shape=jax.ShapeDtypeStruct((bsz, l, dv), BF16),
        scratch_shapes=[pltpu.VMEM((GLA_HEADS, dk // GLA_HEADS, dv // GLA_HEADS), F32)],
        compiler_params=_params("arbitrary", "arbitrary"),
        name="gla_scan_bwd" if reverse else "gla_scan_fwd",
    )(*args)


def _ffn_kernel(*refs, gla_prologue):
    if gla_prologue:
        (o_ref, gate_ref, x_ref, mod_ref, ng_ref, wout_ref, lng0_ref, lnb0_ref,
         wg_ref, wu_ref, wd_ref, lng_ref, lnb_ref, out_ref, hid_ref) = refs
    else:
        h_ref, mod_ref, wg_ref, wu_ref, wd_ref, lng_ref, lnb_ref, out_ref, hid_ref = refs
    m = mod_ref[0]
    hidden = wg_ref.shape[1]
    nsub = out_ref.shape[0] // FFN_SUB

    def head(lo, size):
        rows = slice(lo, lo + size)
        if gla_prologue:
            o = o_ref[rows, :].astype(F32)
            dvh = o.shape[1] // GLA_HEADS
            normed = []
            for hd in range(GLA_HEADS):
                o_h = o[:, hd * dvh:(hd + 1) * dvh]
                ms = jnp.mean(o_h * o_h, axis=-1, keepdims=True)
                normed.append(o_h * lax.rsqrt(ms + 1e-6) * ng_ref[...])
            gated = (jnp.concatenate(normed, axis=1) * gate_ref[rows, :].astype(F32)).astype(BF16)
            y = _dot(gated, wout_ref[...])
            h = _layer_norm(ALPHA * x_ref[rows, :] + m[2:3] * y, lng0_ref[...], lnb0_ref[...])
        else:
            h = h_ref[rows, :]
        af = h * (1.0 + m[4:5]) + m[3:4]
        return h, af.astype(BF16), _lane_fold(af)

    def tail(lo, h, f, after=None):
        res = ALPHA * h + m[5:6] * f
        if after is not None:
            res = _order_after(res, after)
        out_ref[lo:lo + h.shape[0], :] = _layer_norm(res, lng_ref[...], lnb_ref[...])

    head_rows = FFN_SUB // len(FFN_HEAD_AT)
    tail_rows = FFN_SUB // FFN_TAIL_PIECES
    h, a, _ = head(0, FFN_SUB)
    pending = None
    for sub in range(nsub):
        nxt, done_tail = [], 0
        for j, lo in enumerate(range(0, hidden, FFN_COLS)):
            piece = None
            if sub + 1 < nsub and j in FFN_HEAD_AT:
                piece = head((sub + 1) * FFN_SUB + FFN_HEAD_AT.index(j) * head_rows, head_rows)
                nxt.append(piece)
            g = _dot(a, wg_ref[:, lo:lo + FFN_COLS])
            u = _dot(a, wu_ref[:, lo:lo + FFN_COLS])
            prod = _silu(g) * u
            if piece is not None:
                prod = jnp.concatenate([_order_after(prod[0:head_rows], piece[2]), prod[head_rows:]], axis=0)
            hid_ref[sub, :, lo:lo + FFN_COLS] = prod.astype(BF16)
            if pending is not None and piece is None and done_tail < FFN_TAIL_PIECES:
                p_lo, p_h, p_f = pending
                r = slice(done_tail * tail_rows, (done_tail + 1) * tail_rows)
                tail(p_lo + done_tail * tail_rows, p_h[r], p_f[r], after=g)
                done_tail += 1
        assert pending is None or done_tail == FFN_TAIL_PIECES
        pending = (sub * FFN_SUB, h, _dot(hid_ref[sub], wd_ref[...]))
        if nxt:
            h = jnp.concatenate([p[0] for p in nxt], axis=0)
            a = jnp.concatenate([p[1] for p in nxt], axis=0)
    tail(*pending)


def _ffn(acts, mods, consts, tiles_per_batch, gla_prologue):
    n, d = acts[0].shape
    tm = FFN_TILE
    hidden = consts[-5].shape[1]
    row = lambda i: (i, 0)
    mod_spec = pl.BlockSpec((1,) + mods.shape[1:], lambda i: (i // tiles_per_batch, 0, 0))
    in_specs = ([pl.BlockSpec((tm, t.shape[1]), row) for t in acts] + [mod_spec]
                + [_const_spec(t.shape) for t in consts])
    return pl.pallas_call(
        functools.partial(_ffn_kernel, gla_prologue=gla_prologue),
        grid=(n // tm,),
        in_specs=in_specs,
        out_specs=pl.BlockSpec((tm, d), row),
        out_shape=jax.ShapeDtypeStruct((n, d), F32),
        scratch_shapes=[pltpu.VMEM((tm // FFN_SUB, FFN_SUB, hidden), BF16)],
        compiler_params=_params("parallel"),
        name="gla_out_ffn" if gla_prologue else "swiglu_ffn",
    )(*acts, mods, *consts)


def _gmlp_kernel(h_ref, mod_ref, win_ref, vg_ref, vb_ref, ws_ref, bs_ref, wout_ref, lng_ref, lnb_ref,
                 o_ref, z_scr, vn_scr):
    m = mod_ref[0]
    width = z_scr.shape[2]
    gw = width // GM_GROUPS
    nsub = o_ref.shape[0] // GMLP_SUB
    chunks = list(range(0, width, GMLP_COLS))
    starts = [sum(GMLP_PIECES[:i]) for i in range(len(GMLP_PIECES))]
    assert sum(GMLP_PIECES) == GMLP_SUB and len(GMLP_PIECES) == len(chunks) and GM_GROUPS >= 3

    def gelu(z):
        return 0.5 * z * (1.0 + lax.erf(z * (0.5 ** 0.5)))

    def head(sub):
        h = h_ref[sub * GMLP_SUB:(sub + 1) * GMLP_SUB, :]
        return h, (h * (1.0 + m[1:2]) + m[0:1]).astype(BF16)

    def v_project(sub, a, between):
        for j, lo in enumerate(chunks):
            raw = _dot(a, win_ref[:, width + lo:width + lo + GMLP_COLS])
            z_scr[sub, :, lo:lo + GMLP_COLS] = raw
            between(j, raw)

    def normalise_piece(sub, p, after):
        r = slice(starts[p], starts[p] + GMLP_PIECES[p])
        v = gelu(_order_after(z_scr[sub, r, :], after))
        vn_scr[sub, r, :] = _layer_norm(v, vg_ref[...], vb_ref[...]).astype(BF16)

    def mix(sub, a, between):
        event = 0
        u_raw = _dot(a, win_ref[:, 0:gw])
        between(event, u_raw)
        y = None
        for gi in range(GM_GROUPS):
            cols = slice(gi * gw, (gi + 1) * gw)
            u = gelu(u_raw)
            if gi + 1 < GM_GROUPS:
                u_raw = _dot(a, win_ref[:, (gi + 1) * gw:(gi + 2) * gw])
                event += 1
                between(event, u_raw)
            bias = jnp.concatenate([bs_ref[gi]] * (gw // LANES), axis=1)
            gated = []
            for t in range(GMLP_SUB // GM_CHUNK):
                rows = slice(t * GM_CHUNK, (t + 1) * GM_CHUNK)
                s = _dot(ws_ref[gi], vn_scr[sub, rows, cols]) + bias
                gated.append((u[rows] * s).astype(BF16))
            y_g = _dot(jnp.concatenate(gated, axis=0), wout_ref[cols, :])
            y = y_g if y is None else y + y_g
            if gi + 2 < GM_GROUPS:
                event += 1
                between(event, y_g)
        assert event == len(GMLP_PIECES) - 1
        return y

    def tail_piece(sub, h, y, p, after=None):
        r = slice(starts[p], starts[p] + GMLP_PIECES[p])
        res = ALPHA * h[r] + m[2:3] * y[r]
        if after is not None:
            res = _order_after(res, after)
        lo = sub * GMLP_SUB + starts[p]
        o_ref[lo:lo + GMLP_PIECES[p], :] = _layer_norm(res, lng_ref[...], lnb_ref[...])

    assert nsub == 2
    (h0, a0), (h1, a1) = head(0), head(1)
    v_project(0, a0, lambda j, raw: None)
    v_project(1, a1, lambda j, raw: normalise_piece(0, j, raw))
    y0 = mix(0, a0, lambda e, result: normalise_piece(1, e, result))
    y1 = mix(1, a1, lambda e, result: tail_piece(0, h0, y0, e, result))
    for p in range(len(GMLP_PIECES)):
        tail_piece(1, h1, y1, p)


def _gmlp(h2d, mods, consts, tiles_per_batch):
    n, d = h2d.shape
    tm = GMLP_TILE
    width = consts[5].shape[0]
    row = lambda i: (i, 0)
    return pl.pallas_call(
        _gmlp_kernel,
        grid=(n // tm,),
        in_specs=[
            pl.BlockSpec((tm, d), row),
            pl.BlockSpec((1,) + mods.shape[1:], lambda i: (i // tiles_per_batch, 0, 0)),
        ] + [_const_spec(t.shape) for t in consts],
        out_specs=pl.BlockSpec((tm, d), row),
        out_shape=jax.ShapeDtypeStruct((n, d), F32),
        scratch_shapes=[pltpu.VMEM((tm // GMLP_SUB, GMLP_SUB, width), F32),
                        pltpu.VMEM((tm // GMLP_SUB, GMLP_SUB, width), BF16)],
        compiler_params=_params("parallel"),
        name="gmlp_mixer",
    )(h2d, mods, *consts)


def kernel(x, c, ctx, c_ctx, mod_w, mod_b, ln_g, ln_b, gla_w_in, gla_w_decay, gla_b_decay, gla_norm_g,
           gla_w_out, gm_w_in, gm_ln_g, gm_ln_b, gm_w_s, gm_b_s, gm_w_out, ffn_w_gate, ffn_w_up, ffn_w_down):
    bsz, l, d = x.shape
    lc = ctx.shape[1]
    n = bsz * l
    assert bsz + 1 <= COND_ROWS
    dk = gla_w_decay.shape[-1]
    dv = gla_w_out.shape[1]
    q_scale = (dk // GLA_HEADS) ** -0.5
    vec = lambda t: t.reshape(1, -1)

    cond = jnp.concatenate([c, c_ctx[None], jnp.zeros((COND_ROWS - bsz - 1, d), F32)], axis=0)
    mods = _adaln(cond, mod_w, mod_b).reshape(DEPTH, COND_ROWS, 6, d)

    w_in = gla_w_in[0]
    o_a = dk + dv
    o_q = o_a + 2 * GLA_RANK
    w_k, w_v, w_a = w_in[:, :dk], w_in[:, dk:o_a], w_in[:, o_a:o_q]
    w_q, w_r = w_in[:, o_q:o_q + dk], w_in[:, o_q + dk:]
    w_a = jnp.pad(w_a, ((0, 0), (0, LANES - 2 * GLA_RANK)))
    w_lat = jnp.concatenate([w_k, w_q, w_a, w_v, w_r], axis=1).astype(BF16)
    w_ctx = jnp.concatenate([w_k, w_a, w_v], axis=1).astype(BF16)
    wdec = jnp.zeros((2, LANES, dk), F32)
    wdec = wdec.at[0, :GLA_RANK].set(gla_w_decay[0, 0]).at[1, GLA_RANK:2 * GLA_RANK].set(gla_w_decay[0, 1])
    wdec = wdec.astype(BF16)
    bdec = gla_b_decay[0].reshape(2, 1, dk)

    tpb = l // PROJ_TILE
    later = (gla_w_out, gm_w_in, gm_w_out, ffn_w_gate, ffn_w_up, ffn_w_down)
    ((v, gate, qin_f, kin_f, kst_f, dec_f, qin_b, kin_b, kst_b, dec_b),
     (w_out, gm_in, gm_out, wg0, wg1, wu0, wu1, wd0, wd1)) = _gla_proj(
        x.reshape(n, d), mods[0], w_lat, wdec, bdec, lambda i: i // tpb, True, PROJ_TILE, dk, dv, q_scale,
        cast=later)
    (vc, kstc_f, decc_f, kstc_b, decc_b), _ = _gla_proj(
        ctx.reshape(bsz * lc, d), mods[0], w_ctx, wdec, bdec, lambda i: bsz, False, lc, dk, dv, q_scale)
    b3 = lambda t: t.reshape(bsz, l, -1)
    v3, vc3 = b3(v), vc.reshape(bsz, lc, dv)
    o_b = _gla_scan(b3(qin_b), b3(kin_b), kst_b, v3, dec_b, kstc_b, vc3, decc_b, True)
    o = _gla_scan(b3(qin_f), b3(kin_f), kst_f, v3, dec_f, kstc_f, vc3, decc_f, False, prev=o_b)

    ffn_w = ((wg0, wu0, wd0), (wg1, wu1, wd1))
    ffn_consts = lambda i: ffn_w[i] + (vec(ln_g[i, 1]), vec(ln_b[i, 1]))
    gla_consts = (vec(gla_norm_g[0]), w_out, vec(ln_g[0, 0]), vec(ln_b[0, 0]))
    tpf = l // FFN_TILE
    h = _ffn((o.reshape(n, dv), gate, x.reshape(n, d)), mods[0], gla_consts + ffn_consts(0), tpf, True)

    bs = jnp.broadcast_to(gm_b_s[0].T[:, :, None], (GM_GROUPS, GM_CHUNK, LANES))
    gm_consts = (gm_in, vec(gm_ln_g[0]), vec(gm_ln_b[0]), gm_w_s[0].astype(BF16), bs,
                 gm_out, vec(ln_g[1, 0]), vec(ln_b[1, 0]))
    h = _gmlp(h, mods[1], gm_consts, l // GMLP_TILE)
    h = _ffn((h,), mods[1], ffn_consts(1), tpf, False)
    return h.reshape(bsz, l, d)
```

```python
import functools

import jax
import jax.numpy as jnp
from jax import lax
from jax.experimental import pallas as pl
from jax.experimental.pallas import tpu as pltpu

F32 = jnp.float32
BF16 = jnp.bfloat16

DEPTH = 2
ALPHA = (2 * DEPTH) ** 0.25
LOG2E = 1.4426950408889634
GLA_HEADS = 4
GLA_RANK = 16
GLA_GATE_NORM = 16.0
GLA_CHUNK = 64
GM_GROUPS = 4
GM_CHUNK = 128
COND_ROWS = 8
ADALN_COL_BLOCKS = 4
LANES = 128
VMEM_LIMIT = 56 * 1024 * 1024

PROJ_TILE = 512
SCAN_BLOCK = 1024
DECAY_BLOCK = 256
KST_GROUP = 2
FFN_TILE = 1024
FFN_SUB = 512
FFN_HEAD_AT = (3, 6)
FFN_TAIL_PIECES = 8
FFN_COLS = 256
GMLP_TILE = 1024
GMLP_SUB = 512
GMLP_COLS = 512
GMLP_PIECES = (96, 96, 80, 80, 80, 80)


def _dot(a, b):
    return jnp.dot(a, b, preferred_element_type=F32)


def _layer_norm(x, g, b, eps=1e-5):
    mu = jnp.mean(x, axis=-1, keepdims=True)
    xc = x - mu
    var = jnp.mean(xc * xc, axis=-1, keepdims=True)
    return xc * lax.rsqrt(var + eps) * g + b


def _silu(x):
    return x * jax.nn.sigmoid(x)


def _lane_fold(t):
    acc = t[:, 0:LANES]
    for j in range(1, t.shape[1] // LANES):
        acc = acc + t[:, j * LANES:(j + 1) * LANES]
    return acc


def _order_after(x, dep):
    bits = pltpu.bitcast(dep[0:x.shape[0], 0:LANES], jnp.uint32)
    half = jnp.uint32(16)
    zero = pltpu.bitcast(lax.shift_right_logical(lax.shift_right_logical(bits, half), half), F32)
    return jnp.concatenate([x[:, 0:LANES] + zero, x[:, LANES:]], axis=1)


def _params(*sem):
    return pltpu.CompilerParams(dimension_semantics=sem, vmem_limit_bytes=VMEM_LIMIT)


def _const_spec(shape):
    nd = len(shape)
    return pl.BlockSpec(shape, lambda *_: (0,) * nd, pipeline_mode=pl.Buffered(1))


def _adaln_kernel(cond_ref, w_ref, b_ref, o_ref):
    s = _silu(cond_ref[...]).astype(BF16)
    o_ref[0] = _dot(s, w_ref[0].astype(BF16)) + b_ref[0]


def _adaln(cond, mod_w, mod_b):
    depth, d, n = mod_w.shape
    tn = n // ADALN_COL_BLOCKS
    return pl.pallas_call(
        _adaln_kernel,
        grid=(depth, n // tn),
        in_specs=[
            pl.BlockSpec((COND_ROWS, d), lambda i, j: (0, 0)),
            pl.BlockSpec((1, d, tn), lambda i, j: (i, 0, j)),
            pl.BlockSpec((1, 1, tn), lambda i, j: (i, 0, j)),
        ],
        out_specs=pl.BlockSpec((1, COND_ROWS, tn), lambda i, j: (i, 0, j)),
        out_shape=jax.ShapeDtypeStruct((depth, COND_ROWS, n), F32),
        compiler_params=_params("arbitrary", "arbitrary"),
        name="adaln",
    )(cond, mod_w, mod_b.reshape(depth, 1, n))


def _log_sigmoid(z):
    return jnp.minimum(z, 0.0) - jnp.log(1.0 + jnp.exp2(jnp.abs(z) * -LOG2E))


def _chunk_tri(n, reverse):
    row = lax.broadcasted_iota(jnp.int32, (n, n), 0)
    col = lax.broadcasted_iota(jnp.int32, (n, n), 1)
    shift = GLA_CHUNK.bit_length() - 1
    same = jnp.right_shift(row, shift) == jnp.right_shift(col, shift)
    return jnp.where(same & ((col >= row) if reverse else (col <= row)), 1.0, 0.0).astype(BF16)


def _gla_proj_kernel(x_ref, mod_ref, w_ref, wdec_ref, bdec_ref, *refs, latent, dk, dv, q_scale, cast_layers):
    n_src, n_dst = len(cast_layers), sum(cast_layers)
    dsts = iter(refs[len(refs) - n_dst:])
    for src, layers in zip(refs[:n_src], cast_layers):
        for layer in range(layers):
            next(dsts)[0] = src[layer, 0].astype(BF16)
    out_refs = refs[n_src:len(refs) - n_dst]
    m = mod_ref[0]
    a = (x_ref[...] * (1.0 + m[1:2]) + m[0:1]).astype(BF16)
    tm = a.shape[0]
    nblk = tm // DECAY_BLOCK
    ncb = DECAY_BLOCK // GLA_CHUNK
    o_a = 2 * dk if latent else dk
    o_v = o_a + LANES
    if latent:
        v_ref, gate_ref = out_refs[:2]
        dir_refs = (out_refs[2:6], out_refs[6:10])
    else:
        v_ref = out_refs[0]
        dir_refs = ((None, None) + tuple(out_refs[1:3]), (None, None) + tuple(out_refs[3:5]))
    k = _dot(a, w_ref[:, 0:dk])
    if latent:
        q = _dot(a, w_ref[:, dk:2 * dk]) * q_scale
    a_lr = _dot(a, w_ref[:, o_a:o_a + LANES]).astype(BF16)
    z = [_dot(a_lr, wdec_ref[rev]) + bdec_ref[rev] for rev in (0, 1)]
    v_ref[...] = _dot(a, w_ref[:, o_v:o_v + dv]).astype(BF16)
    g_parts = []
    for rev in (0, 1):
        g = _log_sigmoid(z[rev]) * (1.0 / GLA_GATE_NORM)
        g_hi = g.astype(BF16)
        g_parts.append((g_hi, (g - g_hi.astype(F32)).astype(BF16)))
    tris = (_chunk_tri(DECAY_BLOCK, False), _chunk_tri(DECAY_BLOCK, True))
    b = {}
    for rev in (0, 1):
        for blk in range(nblk):
            rows = slice(blk * DECAY_BLOCK, (blk + 1) * DECAY_BLOCK)
            b[rev, blk] = _dot(tris[rev], g_parts[rev][0][rows]) + _dot(tris[rev], g_parts[rev][1][rows])
    if latent:
        gate_ref[...] = _silu(_dot(a, w_ref[:, o_v + dv:o_v + 2 * dv])).astype(BF16)
    for rev in (0, 1):
        qin_ref, kin_ref, kst_ref, dec_ref = dir_refs[rev]
        group_chunks = dec_ref.shape[1]
        for blk in range(nblk):
            rows = slice(blk * DECAY_BLOCK, (blk + 1) * DECAY_BLOCK)
            bb = b[rev, blk] * LOG2E
            if latent:
                qin_ref[rows, :] = (q[rows] * jnp.exp2(bb)).astype(BF16)
                kin_ref[rows, :] = (k[rows] * jnp.exp2(-bb)).astype(BF16)
            ends, kst = [], []
            for c in range(ncb):
                lo = c * GLA_CHUNK
                b_c = bb[lo:lo + GLA_CHUNK]
                b_end = b_c[0:1] if rev else b_c[GLA_CHUNK - 1:GLA_CHUNK]
                kst.append(k[blk * DECAY_BLOCK + lo:blk * DECAY_BLOCK + lo + GLA_CHUNK] * jnp.exp2(b_end - b_c))
                ends.append(b_end)
            for p in range(ncb // KST_GROUP):
                grp = jnp.concatenate(kst[p * KST_GROUP:(p + 1) * KST_GROUP], axis=0)
                kst_ref[blk * (ncb // KST_GROUP) + p] = grp.T.astype(BF16)
            first = blk * ncb
            dec_ref[first // group_chunks, first % group_chunks:first % group_chunks + ncb, :] = jnp.exp2(
                jnp.concatenate(ends, axis=0))


def _gla_proj(x2d, mods, w, wdec, bdec, mod_row, latent, tm, dk, dv, q_scale, cast=()):
    n, d = x2d.shape
    steps = n // tm
    group = min(SCAN_BLOCK, tm)
    row = lambda i: (i, 0)
    row3 = lambda i: (i, 0, 0)
    slabs = [t.reshape(t.shape[0], steps, t.shape[1] // steps, t.shape[2]) for t in cast]
    slab_in = [pl.BlockSpec((t.shape[0], 1) + t.shape[2:], lambda i: (0, i, 0, 0)) for t in slabs]
    slab_out = [(jax.ShapeDtypeStruct(t.shape[1:], BF16), pl.BlockSpec((1,) + t.shape[2:], row3))
                for t in slabs for _ in range(t.shape[0])]
    tok = lambda c: (jax.ShapeDtypeStruct((n, c), BF16), pl.BlockSpec((tm, c), row))
    kst = (jax.ShapeDtypeStruct((n // LANES, dk, LANES), BF16),
           pl.BlockSpec((tm // LANES, dk, LANES), row3))
    dec = (jax.ShapeDtypeStruct((n // group, group // GLA_CHUNK, dk), F32),
           pl.BlockSpec((tm // group, group // GLA_CHUNK, dk), row3))
    per_dir = [tok(dk), tok(dk), kst, dec] if latent else [kst, dec]
    outs = ([tok(dv), tok(dv)] if latent else [tok(dv)]) + per_dir + per_dir
    res = pl.pallas_call(
        functools.partial(_gla_proj_kernel, latent=latent, dk=dk, dv=dv, q_scale=q_scale,
                          cast_layers=tuple(t.shape[0] for t in cast)),
        grid=(steps,),
        in_specs=[
            pl.BlockSpec((tm, d), row),
            pl.BlockSpec((1,) + mods.shape[1:], lambda i: (mod_row(i), 0, 0)),
            _const_spec(w.shape), _const_spec(wdec.shape), _const_spec(bdec.shape),
        ] + slab_in,
        out_specs=[o[1] for o in outs] + [o[1] for o in slab_out],
        out_shape=[o[0] for o in outs] + [o[0] for o in slab_out],
        compiler_params=_params("parallel"),
        name="gla_in_proj" if latent else "gla_ctx_proj",
    )(x2d, mods, w, wdec, bdec, *slabs)
    n_out = len(outs)
    shapes = [t.shape[1:] for t in cast for _ in range(t.shape[0])]
    return res[:n_out], [r.reshape(s) for r, s in zip(res[n_out:], shapes)]


def _decay_columns(dec_row, dvh):
    dkh = dec_row.shape[1]
    dcol = jnp.broadcast_to(dec_row, (dkh, dkh)).T
    return jnp.concatenate([dcol] * (dvh // dkh), axis=1)


def _gla_scan_kernel(*refs, reverse, add_prev):
    qin_ref, kin_ref, kst_ref, v_ref, dec_ref, kstc_ref, vc_ref, decc_ref = refs[:8]
    if add_prev:
        prev_ref, o_ref, s_ref = refs[8:]
    else:
        o_ref, s_ref = refs[8:]
    dkh, dvh = s_ref.shape[1:]
    heads = range(GLA_HEADS)
    ks = [slice(h * dkh, (h + 1) * dkh) for h in heads]
    vs = [slice(h * dvh, (h + 1) * dvh) for h in heads]

    def order(nc):
        return list(range(nc - 1, -1, -1) if reverse else range(nc))

    def rows(c):
        return slice(c * GLA_CHUNK, (c + 1) * GLA_CHUNK)

    lane_chunk = lax.broadcasted_iota(jnp.int32, (dkh, LANES), 1) // GLA_CHUNK

    def state_update(kst_g_ref, v_g_ref, c, h):
        g = c // KST_GROUP
        keys = jnp.where(lane_chunk == c % KST_GROUP, kst_g_ref[g, ks[h], :], jnp.zeros((), BF16))
        return _dot(keys, v_g_ref[0, g * LANES:(g + 1) * LANES, vs[h]])

    @pl.when(pl.program_id(1) == 0)
    def _():
        chunks = order(vc_ref.shape[1] // GLA_CHUNK)
        upd = {(c, h): state_update(kstc_ref, vc_ref, c, h) for c in chunks for h in heads}
        for h in heads:
            s = jnp.zeros((dkh, dvh), F32)
            for c in chunks:
                s = s * _decay_columns(decc_ref[0, c:c + 1, ks[h]], dvh) + upd[c, h]
            s_ref[h] = s

    chunks = order(v_ref.shape[1] // GLA_CHUNK)
    pairs = [(c, h) for c in chunks for h in heads]
    r64 = lax.broadcasted_iota(jnp.int32, (GLA_CHUNK, GLA_CHUNK), 0)
    c64 = lax.broadcasted_iota(jnp.int32, (GLA_CHUNK, GLA_CHUNK), 1)
    mask = (c64 >= r64) if reverse else (c64 <= r64)
    att = {(c, h): lax.dot_general(qin_ref[0, rows(c), ks[h]], kin_ref[0, rows(c), ks[h]],
                                   (((1,), (1,)), ((), ())), preferred_element_type=F32) for c, h in pairs}
    upd = {(c, h): state_update(kst_ref, v_ref, c, h) for c, h in pairs}
    dcol = {(c, h): _decay_columns(dec_ref[0, c:c + 1, ks[h]], dvh) for c, h in pairs}
    lhs = {(c, h): jnp.concatenate([qin_ref[0, rows(c), ks[h]], jnp.where(mask, att[c, h], 0.0).astype(BF16)],
                                   axis=1) for c, h in pairs}
    state = [s_ref[h] for h in heads]
    for c in chunks:
        outs = []
        for h in heads:
            rhs = jnp.concatenate([state[h].astype(BF16), v_ref[0, rows(c), vs[h]]], axis=0)
            outs.append(_dot(lhs[c, h], rhs))
            state[h] = state[h] * dcol[c, h] + upd[c, h]
        o = jnp.concatenate(outs, axis=1)
        if add_prev:
            o = o + prev_ref[0, rows(c), :].astype(F32)
        o_ref[0, rows(c), :] = o.astype(o_ref.dtype)
    for h in heads:
        s_ref[h] = state[h]


def _gla_scan(qin, kin, kst, v, dec, kstc, vc, decc, reverse, prev=None):
    bsz, l, dk = qin.shape
    dv = v.shape[2]
    lc = vc.shape[1]
    tb = SCAN_BLOCK
    nblk = l // tb
    ncb = tb // GLA_CHUNK

    def pos(j):
        return (nblk - 1 - j) if reverse else j

    blk = lambda b, j: (b, pos(j), 0)
    flat = lambda b, j: (b * nblk + pos(j), 0, 0)
    ctx = lambda b, j: (b, 0, 0)
    in_specs = [
        pl.BlockSpec((1, tb, dk), blk), pl.BlockSpec((1, tb, dk), blk),
        pl.BlockSpec((tb // LANES, dk, LANES), flat), pl.BlockSpec((1, tb, dv), blk),
        pl.BlockSpec((1, ncb, dk), flat),
        pl.BlockSpec((lc // LANES, dk, LANES), ctx), pl.BlockSpec((1, lc, dv), ctx),
        pl.BlockSpec((1, lc // GLA_CHUNK, dk), ctx),
    ]
    args = [qin, kin, kst, v, dec.reshape(bsz * nblk, ncb, dk), kstc, vc, decc]
    if prev is not None:
        in_specs.append(pl.BlockSpec((1, tb, dv), blk))
        args.append(prev)
    return pl.pallas_call(
        functools.partial(_gla_scan_kernel, reverse=reverse, add_prev=prev is not None),
        grid=(bsz, nblk),
        in_specs=in_specs,
        out_specs=pl.BlockSpec((1, tb, dv), blk),
        out_shape=jax.ShapeDtypeStruct((bsz, l, dv), BF16),
        scratch_shapes=[pltpu.VMEM((GLA_HEADS, dk // GLA_HEADS, dv // GLA_HEADS), F32)],
        compiler_params=_params("arbitrary", "arbitrary"),
        name="gla_scan_bwd" if reverse else "gla_scan_fwd",
    )(*args)


def _ffn_kernel(*refs, gla_prologue):
    if gla_prologue:
        (o_ref, gate_ref, x_ref, mod_ref, ng_ref, wout_ref, lng0_ref, lnb0_ref,
         wg_ref, wu_ref, wd_ref, lng_ref, lnb_ref, out_ref, hid_ref) = refs
    else:
        h_ref, mod_ref, wg_ref, wu_ref, wd_ref, lng_ref, lnb_ref, out_ref, hid_ref = refs
    m = mod_ref[0]
    hidden = wg_ref.shape[1]
    nsub = out_ref.shape[0] // FFN_SUB

    def head(lo, size):
        rows = slice(lo, lo + size)
        if gla_prologue:
            o = o_ref[rows, :].astype(F32)
            dvh = o.shape[1] // GLA_HEADS
            normed = []
            for hd in range(GLA_HEADS):
                o_h = o[:, hd * dvh:(hd + 1) * dvh]
                ms = jnp.mean(o_h * o_h, axis=-1, keepdims=True)
                normed.append(o_h * lax.rsqrt(ms + 1e-6) * ng_ref[...])
            gated = (jnp.concatenate(normed, axis=1) * gate_ref[rows, :].astype(F32)).astype(BF16)
            y = _dot(gated, wout_ref[...])
            h = _layer_norm(ALPHA * x_ref[rows, :] + m[2:3] * y, lng0_ref[...], lnb0_ref[...])
        else:
            h = h_ref[rows, :]
        af = h * (1.0 + m[4:5]) + m[3:4]
        return h, af.astype(BF16), _lane_fold(af)

    def tail(lo, h, f, after=None):
        res = ALPHA * h + m[5:6] * f
        if after is not None:
            res = _order_after(res, after)
        out_ref[lo:lo + h.shape[0], :] = _layer_norm(res, lng_ref[...], lnb_ref[...])

    head_rows = FFN_SUB // len(FFN_HEAD_AT)
    tail_rows = FFN_SUB // FFN_TAIL_PIECES
    h, a, _ = head(0, FFN_SUB)
    pending = None
    for sub in range(nsub):
        nxt, done_tail = [], 0
        for j, lo in enumerate(range(0, hidden, FFN_COLS)):
            piece = None
            if sub + 1 < nsub and j in FFN_HEAD_AT:
                piece = head((sub + 1) * FFN_SUB + FFN_HEAD_AT.index(j) * head_rows, head_rows)
                nxt.append(piece)
            g = _dot(a, wg_ref[:, lo:lo + FFN_COLS])
            u = _dot(a, wu_ref[:, lo:lo + FFN_COLS])
            prod = _silu(g) * u
            if piece is not None:
                prod = jnp.concatenate([_order_after(prod[0:head_rows], piece[2]), prod[head_rows:]], axis=0)
            hid_ref[sub, :, lo:lo + FFN_COLS] = prod.astype(BF16)
            if pending is not None and piece is None and done_tail < FFN_TAIL_PIECES:
                p_lo, p_h, p_f = pending
                r = slice(done_tail * tail_rows, (done_tail + 1) * tail_rows)
                tail(p_lo + done_tail * tail_rows, p_h[r], p_f[r], after=g)
                done_tail += 1
        assert pending is None or done_tail == FFN_TAIL_PIECES
        pending = (sub * FFN_SUB, h, _dot(hid_ref[sub], wd_ref[...]))
        if nxt:
            h = jnp.concatenate([p[0] for p in nxt], axis=0)
            a = jnp.concatenate([p[1] for p in nxt], axis=0)
    tail(*pending)


def _ffn(acts, mods, consts, tiles_per_batch, gla_prologue):
    n, d = acts[0].shape
    tm = FFN_TILE
    hidden = consts[-5].shape[1]
    row = lambda i: (i, 0)
    mod_spec = pl.BlockSpec((1,) + mods.shape[1:], lambda i: (i // tiles_per_batch, 0, 0))
    in_specs = ([pl.BlockSpec((tm, t.shape[1]), row) for t in acts] + [mod_spec]
                + [_const_spec(t.shape) for t in consts])
    return pl.pallas_call(
        functools.partial(_ffn_kernel, gla_prologue=gla_prologue),
        grid=(n // tm,),
        in_specs=in_specs,
        out_specs=pl.BlockSpec((tm, d), row),
        out_shape=jax.ShapeDtypeStruct((n, d), F32),
        scratch_shapes=[pltpu.VMEM((tm // FFN_SUB, FFN_SUB, hidden), BF16)],
        compiler_params=_params("parallel"),
        name="gla_out_ffn" if gla_prologue else "swiglu_ffn",
    )(*acts, mods, *consts)


def _gmlp_kernel(h_ref, mod_ref, win_ref, vg_ref, vb_ref, ws_ref, bs_ref, wout_ref, lng_ref, lnb_ref,
                 o_ref, z_scr, vn_scr):
    m = mod_ref[0]
    width = z_scr.shape[2]
    gw = width // GM_GROUPS
    nsub = o_ref.shape[0] // GMLP_SUB
    chunks = list(range(0, width, GMLP_COLS))
    starts = [sum(GMLP_PIECES[:i]) for i in range(len(GMLP_PIECES))]
    assert sum(GMLP_PIECES) == GMLP_SUB and len(GMLP_PIECES) == len(chunks) and GM_GROUPS >= 3

    def gelu(z):
        return 0.5 * z * (1.0 + lax.erf(z * (0.5 ** 0.5)))

    def head(sub):
        h = h_ref[sub * GMLP_SUB:(sub + 1) * GMLP_SUB, :]
        return h, (h * (1.0 + m[1:2]) + m[0:1]).astype(BF16)

    def v_project(sub, a, between):
        for j, lo in enumerate(chunks):
            raw = _dot(a, win_ref[:, width + lo:width + lo + GMLP_COLS])
            z_scr[sub, :, lo:lo + GMLP_COLS] = raw
            between(j, raw)

    def normalise_piece(sub, p, after):
        r = slice(starts[p], starts[p] + GMLP_PIECES[p])
        v = gelu(_order_after(z_scr[sub, r, :], after))
        vn_scr[sub, r, :] = _layer_norm(v, vg_ref[...], vb_ref[...]).astype(BF16)

    def mix(sub, a, between):
        event = 0
        u_raw = _dot(a, win_ref[:, 0:gw])
        between(event, u_raw)
        y = None
        for gi in range(GM_GROUPS):
            cols = slice(gi * gw, (gi + 1) * gw)
            u = gelu(u_raw)
            if gi + 1 < GM_GROUPS:
                u_raw = _dot(a, win_ref[:, (gi + 1) * gw:(gi + 2) * gw])
                event += 1
                between(event, u_raw)
            bias = jnp.concatenate([bs_ref[gi]] * (gw // LANES), axis=1)
            gated = []
            for t in range(GMLP_SUB // GM_CHUNK):
                rows = slice(t * GM_CHUNK, (t + 1) * GM_CHUNK)
                s = _dot(ws_ref[gi], vn_scr[sub, rows, cols]) + bias
                gated.append((u[rows] * s).astype(BF16))
            y_g = _dot(jnp.concatenate(gated, axis=0), wout_ref[cols, :])
            y = y_g if y is None else y + y_g
            if gi + 2 < GM_GROUPS:
                event += 1
                between(event, y_g)
        assert event == len(GMLP_PIECES) - 1
        return y

    def tail_piece(sub, h, y, p, after=None):
        r = slice(starts[p], starts[p] + GMLP_PIECES[p])
        res = ALPHA * h[r] + m[2:3] * y[r]
        if after is not None:
            res = _order_after(res, after)
        lo = sub * GMLP_SUB + starts[p]
        o_ref[lo:lo + GMLP_PIECES[p], :] = _layer_norm(res, lng_ref[...], lnb_ref[...])

    assert nsub == 2
    (h0, a0), (h1, a1) = head(0), head(1)
    v_project(0, a0, lambda j, raw: None)
    v_project(1, a1, lambda j, raw: normalise_piece(0, j, raw))
    y0 = mix(0, a0, lambda e, result: normalise_piece(1, e, result))
    y1 = mix(1, a1, lambda e, result: tail_piece(0, h0, y0, e, result))
    for p in range(len(GMLP_PIECES)):
        tail_piece(1, h1, y1, p)


def _gmlp(h2d, mods, consts, tiles_per_batch):
    n, d = h2d.shape
    tm = GMLP_TILE
    width = consts[5].shape[0]
    row = lambda i: (i, 0)
    return pl.pallas_call(
        _gmlp_kernel,
        grid=(n // tm,),
        in_specs=[
            pl.BlockSpec((tm, d), row),
            pl.BlockSpec((1,) + mods.shape[1:], lambda i: (i // tiles_per_batch, 0, 0)),
        ] + [_const_spec(t.shape) for t in consts],
        out_specs=pl.BlockSpec((tm, d), row),
        out_shape=jax.ShapeDtypeStruct((n, d), F32),
        scratch_shapes=[pltpu.VMEM((tm // GMLP_SUB, GMLP_SUB, width), F32),
                        pltpu.VMEM((tm // GMLP_SUB, GMLP_SUB, width), BF16)],
        compiler_params=_params("parallel"),
        name="gmlp_mixer",
    )(h2d, mods, *consts)


def kernel(x, c, ctx, c_ctx, mod_w, mod_b, ln_g, ln_b, gla_w_in, gla_w_decay, gla_b_decay, gla_norm_g,
           gla_w_out, gm_w_in, gm_ln_g, gm_ln_b, gm_w_s, gm_b_s, gm_w_out, ffn_w_gate, ffn_w_up, ffn_w_down):
    bsz, l, d = x.shape
    lc = ctx.shape[1]
    n = bsz * l
    assert bsz + 1 <= COND_ROWS
    dk = gla_w_decay.shape[-1]
    dv = gla_w_out.shape[1]
    q_scale = (dk // GLA_HEADS) ** -0.5
    vec = lambda t: t.reshape(1, -1)

    cond = jnp.concatenate([c, c_ctx[None], jnp.zeros((COND_ROWS - bsz - 1, d), F32)], axis=0)
    mods = _adaln(cond, mod_w, mod_b).reshape(DEPTH, COND_ROWS, 6, d)

    w_in = gla_w_in[0]
    o_a = dk + dv
    o_q = o_a + 2 * GLA_RANK
    w_k, w_v, w_a = w_in[:, :dk], w_in[:, dk:o_a], w_in[:, o_a:o_q]
    w_q, w_r = w_in[:, o_q:o_q + dk], w_in[:, o_q + dk:]
    w_a = jnp.pad(w_a, ((0, 0), (0, LANES - 2 * GLA_RANK)))
    w_lat = jnp.concatenate([w_k, w_q, w_a, w_v, w_r], axis=1).astype(BF16)
    w_ctx = jnp.concatenate([w_k, w_a, w_v], axis=1).astype(BF16)
    wdec = jnp.zeros((2, LANES, dk), F32)
    wdec = wdec.at[0, :GLA_RANK].set(gla_w_decay[0, 0]).at[1, GLA_RANK:2 * GLA_RANK].set(gla_w_decay[0, 1])
    wdec = wdec.astype(BF16)
    bdec = gla_b_decay[0].reshape(2, 1, dk)

    tpb = l // PROJ_TILE
    later = (gla_w_out, gm_w_in, gm_w_out, ffn_w_gate, ffn_w_up, ffn_w_down)
    ((v, gate, qin_f, kin_f, kst_f, dec_f, qin_b, kin_b, kst_b, dec_b),
     (w_out, gm_in, gm_out, wg0, wg1, wu0, wu1, wd0, wd1)) = _gla_proj(
        x.reshape(n, d), mods[0], w_lat, wdec, bdec, lambda i: i // tpb, True, PROJ_TILE, dk, dv, q_scale,
        cast=later)
    (vc, kstc_f, decc_f, kstc_b, decc_b), _ = _gla_proj(
        ctx.reshape(bsz * lc, d), mods[0], w_ctx, wdec, bdec, lambda i: bsz, False, lc, dk, dv, q_scale)
    b3 = lambda t: t.reshape(bsz, l, -1)
    v3, vc3 = b3(v), vc.reshape(bsz, lc, dv)
    o_b = _gla_scan(b3(qin_b), b3(kin_b), kst_b, v3, dec_b, kstc_b, vc3, decc_b, True)
    o = _gla_scan(b3(qin_f), b3(kin_f), kst_f, v3, dec_f, kstc_f, vc3, decc_f, False, prev=o_b)

    ffn_w = ((wg0, wu0, wd0), (wg1, wu1, wd1))
    ffn_consts = lambda i: ffn_w[i] + (vec(ln_g[i, 1]), vec(ln_b[i, 1]))
    gla_consts = (vec(gla_norm_g[0]), w_out, vec(ln_g[0, 0]), vec(ln_b[0, 0]))
    tpf = l // FFN_TILE
    h = _ffn((o.reshape(n, dv), gate, x.reshape(n, d)), mods[0], gla_consts + ffn_consts(0), tpf, True)

    bs = jnp.broadcast_to(gm_b_s[0].T[:, :, None], (GM_GROUPS, GM_CHUNK, LANES))
    gm_consts = (gm_in, vec(gm_ln_g[0]), vec(gm_ln_b[0]), gm_w_s[0].astype(BF16), bs,
                 gm_out, vec(ln_g[1, 0]), vec(ln_b[1, 0]))
    h = _gmlp(h, mods[1], gm_consts, l // GMLP_TILE)
    h = _ffn((h,), mods[1], ffn_consts(1), tpf, False)
    return h.reshape(bsz, l, d)
```

```python
import functools

import jax
import jax.numpy as jnp
from jax import lax
from jax.experimental import pallas as pl
from jax.experimental.pallas import tpu as pltpu

F32 = jnp.float32
BF16 = jnp.bfloat16

DEPTH = 2
ALPHA = (2 * DEPTH) ** 0.25
LOG2E = 1.4426950408889634
GLA_HEADS = 4
GLA_RANK = 16
GLA_GATE_NORM = 16.0
GLA_CHUNK = 64
GM_GROUPS = 4
GM_CHUNK = 128
COND_ROWS = 8
ADALN_COL_BLOCKS = 2
LANES = 128
VMEM_LIMIT = 56 * 1024 * 1024

PROJ_TILE = 512
SCAN_BLOCK = 1024
DECAY_BLOCK = 256
KST_GROUP = 2
FFN_TILE = 1024
FFN_SUB = 512
FFN_HEAD_AT = (3, 6)
FFN_TAIL_PIECES = 8
FFN_COLS = 256
GMLP_TILE = 1024
GMLP_SUB = 512
GMLP_COLS = 512
GMLP_PIECES = (96, 96, 80, 80, 80, 80)


def _dot(a, b):
    return jnp.dot(a, b, preferred_element_type=F32)


def _layer_norm(x, g, b, eps=1e-5):
    mu = jnp.mean(x, axis=-1, keepdims=True)
    xc = x - mu
    var = jnp.mean(xc * xc, axis=-1, keepdims=True)
    return xc * lax.rsqrt(var + eps) * g + b


def _silu(x):
    return x * jax.nn.sigmoid(x)


def _lane_fold(t):
    acc = t[:, 0:LANES]
    for j in range(1, t.shape[1] // LANES):
        acc = acc + t[:, j * LANES:(j + 1) * LANES]
    return acc


def _order_after(x, dep):
    bits = pltpu.bitcast(dep[0:x.shape[0], 0:LANES], jnp.uint32)
    half = jnp.uint32(16)
    zero = pltpu.bitcast(lax.shift_right_logical(lax.shift_right_logical(bits, half), half), F32)
    return jnp.concatenate([x[:, 0:LANES] + zero, x[:, LANES:]], axis=1)


def _params(*sem):
    return pltpu.CompilerParams(dimension_semantics=sem, vmem_limit_bytes=VMEM_LIMIT)


def _const_spec(shape):
    nd = len(shape)
    return pl.BlockSpec(shape, lambda *_: (0,) * nd, pipeline_mode=pl.Buffered(1))


def _adaln_kernel(cond_ref, w_ref, b_ref, o_ref):
    s = _silu(cond_ref[...]).astype(BF16)
    o_ref[0] = _dot(s, w_ref[0].astype(BF16)) + b_ref[0]


def _adaln(cond, mod_w, mod_b):
    depth, d, n = mod_w.shape
    tn = n // ADALN_COL_BLOCKS
    return pl.pallas_call(
        _adaln_kernel,
        grid=(depth, n // tn),
        in_specs=[
            pl.BlockSpec((COND_ROWS, d), lambda i, j: (0, 0)),
            pl.BlockSpec((1, d, tn), lambda i, j: (i, 0, j)),
            pl.BlockSpec((1, 1, tn), lambda i, j: (i, 0, j)),
        ],
        out_specs=pl.BlockSpec((1, COND_ROWS, tn), lambda i, j: (i, 0, j)),
        out_shape=jax.ShapeDtypeStruct((depth, COND_ROWS, n), F32),
        compiler_params=_params("arbitrary", "arbitrary"),
        name="adaln",
    )(cond, mod_w, mod_b.reshape(depth, 1, n))


def _log_sigmoid(z):
    return jnp.minimum(z, 0.0) - jnp.log(1.0 + jnp.exp2(jnp.abs(z) * -LOG2E))


def _chunk_tri(n, reverse):
    row = lax.broadcasted_iota(jnp.int32, (n, n), 0)
    col = lax.broadcasted_iota(jnp.int32, (n, n), 1)
    shift = GLA_CHUNK.bit_length() - 1
    same = jnp.right_shift(row, shift) == jnp.right_shift(col, shift)
    return jnp.where(same & ((col >= row) if reverse else (col <= row)), 1.0, 0.0).astype(BF16)


def _gla_proj_kernel(x_ref, mod_ref, w_ref, wdec_ref, bdec_ref, *refs, latent, dk, dv, q_scale, cast_layers):
    n_src, n_dst = len(cast_layers), sum(cast_layers)
    dsts = iter(refs[len(refs) - n_dst:])
    for src, layers in zip(refs[:n_src], cast_layers):
        for layer in range(layers):
            next(dsts)[0] = src[layer, 0].astype(BF16)
    out_refs = refs[n_src:len(refs) - n_dst]
    m = mod_ref[0]
    a = (x_ref[...] * (1.0 + m[1:2]) + m[0:1]).astype(BF16)
    tm = a.shape[0]
    nblk = tm // DECAY_BLOCK
    ncb = DECAY_BLOCK // GLA_CHUNK
    o_a = 2 * dk if latent else dk
    o_v = o_a + LANES
    if latent:
        v_ref, gate_ref = out_refs[:2]
        dir_refs = (out_refs[2:6], out_refs[6:10])
    else:
        v_ref = out_refs[0]
        dir_refs = ((None, None) + tuple(out_refs[1:3]), (None, None) + tuple(out_refs[3:5]))
    k = _dot(a, w_ref[:, 0:dk])
    if latent:
        q = _dot(a, w_ref[:, dk:2 * dk]) * q_scale
    a_lr = _dot(a, w_ref[:, o_a:o_a + LANES]).astype(BF16)
    z = [_dot(a_lr, wdec_ref[rev]) + bdec_ref[rev] for rev in (0, 1)]
    v_ref[...] = _dot(a, w_ref[:, o_v:o_v + dv]).astype(BF16)
    g_parts = []
    for rev in (0, 1):
        g = _log_sigmoid(z[rev]) * (1.0 / GLA_GATE_NORM)
        g_hi = g.astype(BF16)
        g_parts.append((g_hi, (g - g_hi.astype(F32)).astype(BF16)))
    tris = (_chunk_tri(DECAY_BLOCK, False), _chunk_tri(DECAY_BLOCK, True))
    b = {}
    for rev in (0, 1):
        for blk in range(nblk):
            rows = slice(blk * DECAY_BLOCK, (blk + 1) * DECAY_BLOCK)
            b[rev, blk] = _dot(tris[rev], g_parts[rev][0][rows]) + _dot(tris[rev], g_parts[rev][1][rows])
    if latent:
        gate_ref[...] = _silu(_dot(a, w_ref[:, o_v + dv:o_v + 2 * dv])).astype(BF16)
    for rev in (0, 1):
        qin_ref, kin_ref, kst_ref, dec_ref = dir_refs[rev]
        group_chunks = dec_ref.shape[1]
        for blk in range(nblk):
            rows = slice(blk * DECAY_BLOCK, (blk + 1) * DECAY_BLOCK)
            bb = b[rev, blk] * LOG2E
            if latent:
                qin_ref[rows, :] = (q[rows] * jnp.exp2(bb)).astype(BF16)
                kin_ref[rows, :] = (k[rows] * jnp.exp2(-bb)).astype(BF16)
            ends, kst = [], []
            for c in range(ncb):
                lo = c * GLA_CHUNK
                b_c = bb[lo:lo + GLA_CHUNK]
                b_end = b_c[0:1] if rev else b_c[GLA_CHUNK - 1:GLA_CHUNK]
                kst.append(k[blk * DECAY_BLOCK + lo:blk * DECAY_BLOCK + lo + GLA_CHUNK] * jnp.exp2(b_end - b_c))
                ends.append(b_end)
            for p in range(ncb // KST_GROUP):
                grp = jnp.concatenate(kst[p * KST_GROUP:(p + 1) * KST_GROUP], axis=0)
                kst_ref[blk * (ncb // KST_GROUP) + p] = grp.T.astype(BF16)
            first = blk * ncb
            dec_ref[first // group_chunks, first % group_chunks:first % group_chunks + ncb, :] = jnp.exp2(
                jnp.concatenate(ends, axis=0))


def _gla_proj(x2d, mods, w, wdec, bdec, mod_row, latent, tm, dk, dv, q_scale, cast=()):
    n, d = x2d.shape
    steps = n // tm
    group = min(SCAN_BLOCK, tm)
    row = lambda i: (i, 0)
    row3 = lambda i: (i, 0, 0)
    slabs = [t.reshape(t.shape[0], steps, t.shape[1] // steps, t.shape[2]) for t in cast]
    slab_in = [pl.BlockSpec((t.shape[0], 1) + t.shape[2:], lambda i: (0, i, 0, 0)) for t in slabs]
    slab_out = [(jax.ShapeDtypeStruct(t.shape[1:], BF16), pl.BlockSpec((1,) + t.shape[2:], row3))
                for t in slabs for _ in range(t.shape[0])]
    tok = lambda c: (jax.ShapeDtypeStruct((n, c), BF16), pl.BlockSpec((tm, c), row))
    kst = (jax.ShapeDtypeStruct((n // LANES, dk, LANES), BF16),
           pl.BlockSpec((tm // LANES, dk, LANES), row3))
    dec = (jax.ShapeDtypeStruct((n // group, group // GLA_CHUNK, dk), F32),
           pl.BlockSpec((tm // group, group // GLA_CHUNK, dk), row3))
    per_dir = [tok(dk), tok(dk), kst, dec] if latent else [kst, dec]
    outs = ([tok(dv), tok(dv)] if latent else [tok(dv)]) + per_dir + per_dir
    res = pl.pallas_call(
        functools.partial(_gla_proj_kernel, latent=latent, dk=dk, dv=dv, q_scale=q_scale,
                          cast_layers=tuple(t.shape[0] for t in cast)),
        grid=(steps,),
        in_specs=[
            pl.BlockSpec((tm, d), row),
            pl.BlockSpec((1,) + mods.shape[1:], lambda i: (mod_row(i), 0, 0)),
            _const_spec(w.shape), _const_spec(wdec.shape), _const_spec(bdec.shape),
        ] + slab_in,
        out_specs=[o[1] for o in outs] + [o[1] for o in slab_out],
        out_shape=[o[0] for o in outs] + [o[0] for o in slab_out],
        compiler_params=_params("parallel"),
        name="gla_in_proj" if latent else "gla_ctx_proj",
    )(x2d, mods, w, wdec, bdec, *slabs)
    n_out = len(outs)
    shapes = [t.shape[1:] for t in cast for _ in range(t.shape[0])]
    return res[:n_out], [r.reshape(s) for r, s in zip(res[n_out:], shapes)]


def _decay_columns(dec_row, dvh):
    dkh = dec_row.shape[1]
    dcol = jnp.broadcast_to(dec_row, (dkh, dkh)).T
    return jnp.concatenate([dcol] * (dvh // dkh), axis=1)


def _gla_scan_kernel(*refs, reverse, add_prev):
    qin_ref, kin_ref, kst_ref, v_ref, dec_ref, kstc_ref, vc_ref, decc_ref = refs[:8]
    if add_prev:
        prev_ref, o_ref, s_ref = refs[8:]
    else:
        o_ref, s_ref = refs[8:]
    dkh, dvh = s_ref.shape[1:]
    heads = range(GLA_HEADS)
    ks = [slice(h * dkh, (h + 1) * dkh) for h in heads]
    vs = [slice(h * dvh, (h + 1) * dvh) for h in heads]

    def order(nc):
        return list(range(nc - 1, -1, -1) if reverse else range(nc))

    def rows(c):
        return slice(c * GLA_CHUNK, (c + 1) * GLA_CHUNK)

    lane_chunk = lax.broadcasted_iota(jnp.int32, (dkh, LANES), 1) // GLA_CHUNK

    def state_update(kst_g_ref, v_g_ref, c, h):
        g = c // KST_GROUP
        keys = jnp.where(lane_chunk == c % KST_GROUP, kst_g_ref[g, ks[h], :], jnp.zeros((), BF16))
        return _dot(keys, v_g_ref[0, g * LANES:(g + 1) * LANES, vs[h]])

    @pl.when(pl.program_id(1) == 0)
    def _():
        chunks = order(vc_ref.shape[1] // GLA_CHUNK)
        upd = {(c, h): state_update(kstc_ref, vc_ref, c, h) for c in chunks for h in heads}
        for h in heads:
            s = jnp.zeros((dkh, dvh), F32)
            for c in chunks:
                s = s * _decay_columns(decc_ref[0, c:c + 1, ks[h]], dvh) + upd[c, h]
            s_ref[h] = s

    chunks = order(v_ref.shape[1] // GLA_CHUNK)
    pairs = [(c, h) for c in chunks for h in heads]
    r64 = lax.broadcasted_iota(jnp.int32, (GLA_CHUNK, GLA_CHUNK), 0)
    c64 = lax.broadcasted_iota(jnp.int32, (GLA_CHUNK, GLA_CHUNK), 1)
    mask = (c64 >= r64) if reverse else (c64 <= r64)
    att = {(c, h): lax.dot_general(qin_ref[0, rows(c), ks[h]], kin_ref[0, rows(c), ks[h]],
                                   (((1,), (1,)), ((), ())), preferred_element_type=F32) for c, h in pairs}
    upd = {(c, h): state_update(kst_ref, v_ref, c, h) for c, h in pairs}
    dcol = {(c, h): _decay_columns(dec_ref[0, c:c + 1, ks[h]], dvh) for c, h in pairs}
    lhs = {(c, h): jnp.concatenate([qin_ref[0, rows(c), ks[h]], jnp.where(mask, att[c, h], 0.0).astype(BF16)],
                                   axis=1) for c, h in pairs}
    state = [s_ref[h] for h in heads]
    for c in chunks:
        outs = []
        for h in heads:
            rhs = jnp.concatenate([state[h].astype(BF16), v_ref[0, rows(c), vs[h]]], axis=0)
            outs.append(_dot(lhs[c, h], rhs))
            state[h] = state[h] * dcol[c, h] + upd[c, h]
        o = jnp.concatenate(outs, axis=1)
        if add_prev:
            o = o + prev_ref[0, rows(c), :].astype(F32)
        o_ref[0, rows(c), :] = o.astype(o_ref.dtype)
    for h in heads:
        s_ref[h] = state[h]


def _gla_scan(qin, kin, kst, v, dec, kstc, vc, decc, reverse, prev=None):
    bsz, l, dk = qin.shape
    dv = v.shape[2]
    lc = vc.shape[1]
    tb = SCAN_BLOCK
    nblk = l // tb
    ncb = tb // GLA_CHUNK

    def pos(j):
        return (nblk - 1 - j) if reverse else j

    blk = lambda b, j: (b, pos(j), 0)
    flat = lambda b, j: (b * nblk + pos(j), 0, 0)
    ctx = lambda b, j: (b, 0, 0)
    in_specs = [
        pl.BlockSpec((1, tb, dk), blk), pl.BlockSpec((1, tb, dk), blk),
        pl.BlockSpec((tb // LANES, dk, LANES), flat), pl.BlockSpec((1, tb, dv), blk),
        pl.BlockSpec((1, ncb, dk), flat),
        pl.BlockSpec((lc // LANES, dk, LANES), ctx), pl.BlockSpec((1, lc, dv), ctx),
        pl.BlockSpec((1, lc // GLA_CHUNK, dk), ctx),
    ]
    args = [qin, kin, kst, v, dec.reshape(bsz * nblk, ncb, dk), kstc, vc, decc]
    if prev is not None:
        in_specs.append(pl.BlockSpec((1, tb, dv), blk))
        args.append(prev)
    return pl.pallas_call(
        functools.partial(_gla_scan_kernel, reverse=reverse, add_prev=prev is not None),
        grid=(bsz, nblk),
        in_specs=in_specs,
        out_specs=pl.BlockSpec((1, tb, dv), blk),
        out_shape=jax.ShapeDtypeStruct((bsz, l, dv), BF16),
        scratch_shapes=[pltpu.VMEM((GLA_HEADS, dk // GLA_HEADS, dv // GLA_HEADS), F32)],
        compiler_params=_params("arbitrary", "arbitrary"),
        name="gla_scan_bwd" if reverse else "gla_scan_fwd",
    )(*args)


def _ffn_kernel(*refs, gla_prologue):
    if gla_prologue:
        (o_ref, gate_ref, x_ref, mod_ref, ng_ref, wout_ref, lng0_ref, lnb0_ref,
         wg_ref, wu_ref, wd_ref, lng_ref, lnb_ref, out_ref, hid_ref) = refs
    else:
        h_ref, mod_ref, wg_ref, wu_ref, wd_ref, lng_ref, lnb_ref, out_ref, hid_ref = refs
    m = mod_ref[0]
    hidden = wg_ref.shape[1]
    nsub = out_ref.shape[0] // FFN_SUB

    def head(lo, size):
        rows = slice(lo, lo + size)
        if gla_prologue:
            o = o_ref[rows, :].astype(F32)
            dvh = o.shape[1] // GLA_HEADS
            normed = []
            for hd in range(GLA_HEADS):
                o_h = o[:, hd * dvh:(hd + 1) * dvh]
                ms = jnp.mean(o_h * o_h, axis=-1, keepdims=True)
                normed.append(o_h * lax.rsqrt(ms + 1e-6) * ng_ref[...])
            gated = (jnp.concatenate(normed, axis=1) * gate_ref[rows, :].astype(F32)).astype(BF16)
            y = _dot(gated, wout_ref[...])
            h = _layer_norm(ALPHA * x_ref[rows, :] + m[2:3] * y, lng0_ref[...], lnb0_ref[...])
        else:
            h = h_ref[rows, :]
        af = h * (1.0 + m[4:5]) + m[3:4]
        return h, af.astype(BF16), _lane_fold(af)

    def tail(lo, h, f, after=None):
        res = ALPHA * h + m[5:6] * f
        if after is not None:
            res = _order_after(res, after)
        out_ref[lo:lo + h.shape[0], :] = _layer_norm(res, lng_ref[...], lnb_ref[...])

    head_rows = FFN_SUB // len(FFN_HEAD_AT)
    tail_rows = FFN_SUB // FFN_TAIL_PIECES
    h, a, _ = head(0, FFN_SUB)
    pending = None
    for sub in range(nsub):
        nxt, done_tail = [], 0
        for j, lo in enumerate(range(0, hidden, FFN_COLS)):
            piece = None
            if sub + 1 < nsub and j in FFN_HEAD_AT:
                piece = head((sub + 1) * FFN_SUB + FFN_HEAD_AT.index(j) * head_rows, head_rows)
                nxt.append(piece)
            g = _dot(a, wg_ref[:, lo:lo + FFN_COLS])
            u = _dot(a, wu_ref[:, lo:lo + FFN_COLS])
            prod = _silu(g) * u
            if piece is not None:
                prod = jnp.concatenate([_order_after(prod[0:head_rows], piece[2]), prod[head_rows:]], axis=0)
            hid_ref[sub, :, lo:lo + FFN_COLS] = prod.astype(BF16)
            if pending is not None and piece is None and done_tail < FFN_TAIL_PIECES:
                p_lo, p_h, p_f = pending
                r = slice(done_tail * tail_rows, (done_tail + 1) * tail_rows)
                tail(p_lo + done_tail * tail_rows, p_h[r], p_f[r], after=g)
                done_tail += 1
        assert pending is None or done_tail == FFN_TAIL_PIECES
        pending = (sub * FFN_SUB, h, _dot(hid_ref[sub], wd_ref[...]))
        if nxt:
            h = jnp.concatenate([p[0] for p in nxt], axis=0)
            a = jnp.concatenate([p[1] for p in nxt], axis=0)
    tail(*pending)


def _ffn(acts, mods, consts, tiles_per_batch, gla_prologue):
    n, d = acts[0].shape
    tm = FFN_TILE
    hidden = consts[-5].shape[1]
    row = lambda i: (i, 0)
    mod_spec = pl.BlockSpec((1,) + mods.shape[1:], lambda i: (i // tiles_per_batch, 0, 0))
    in_specs = ([pl.BlockSpec((tm, t.shape[1]), row) for t in acts] + [mod_spec]
                + [_const_spec(t.shape) for t in consts])
    return pl.pallas_call(
        functools.partial(_ffn_kernel, gla_prologue=gla_prologue),
        grid=(n // tm,),
        in_specs=in_specs,
        out_specs=pl.BlockSpec((tm, d), row),
        out_shape=jax.ShapeDtypeStruct((n, d), F32),
        scratch_shapes=[pltpu.VMEM((tm // FFN_SUB, FFN_SUB, hidden), BF16)],
        compiler_params=_params("parallel"),
        name="gla_out_ffn" if gla_prologue else "swiglu_ffn",
    )(*acts, mods, *consts)


def _gmlp_kernel(h_ref, mod_ref, win_ref, vg_ref, vb_ref, ws_ref, bs_ref, wout_ref, lng_ref, lnb_ref,
                 o_ref, z_scr, vn_scr):
    m = mod_ref[0]
    width = z_scr.shape[2]
    gw = width // GM_GROUPS
    nsub = o_ref.shape[0] // GMLP_SUB
    chunks = list(range(0, width, GMLP_COLS))
    starts = [sum(GMLP_PIECES[:i]) for i in range(len(GMLP_PIECES))]
    assert sum(GMLP_PIECES) == GMLP_SUB and len(GMLP_PIECES) == len(chunks) and GM_GROUPS >= 3

    def gelu(z):
        return 0.5 * z * (1.0 + lax.erf(z * (0.5 ** 0.5)))

    def head(sub):
        h = h_ref[sub * GMLP_SUB:(sub + 1) * GMLP_SUB, :]
        return h, (h * (1.0 + m[1:2]) + m[0:1]).astype(BF16)

    def v_project(sub, a, between):
        for j, lo in enumerate(chunks):
            raw = _dot(a, win_ref[:, width + lo:width + lo + GMLP_COLS])
            z_scr[sub, :, lo:lo + GMLP_COLS] = raw
            between(j, raw)

    def normalise_piece(sub, p, after):
        r = slice(starts[p], starts[p] + GMLP_PIECES[p])
        v = gelu(_order_after(z_scr[sub, r, :], after))
        vn_scr[sub, r, :] = _layer_norm(v, vg_ref[...], vb_ref[...]).astype(BF16)

    def mix(sub, a, between):
        event = 0
        u_raw = _dot(a, win_ref[:, 0:gw])
        between(event, u_raw)
        y = None
        for gi in range(GM_GROUPS):
            cols = slice(gi * gw, (gi + 1) * gw)
            u = gelu(u_raw)
            if gi + 1 < GM_GROUPS:
                u_raw = _dot(a, win_ref[:, (gi + 1) * gw:(gi + 2) * gw])
                event += 1
                between(event, u_raw)
            bias = jnp.concatenate([bs_ref[gi]] * (gw // LANES), axis=1)
            gated = []
            for t in range(GMLP_SUB // GM_CHUNK):
                rows = slice(t * GM_CHUNK, (t + 1) * GM_CHUNK)
                s = _dot(ws_ref[gi], vn_scr[sub, rows, cols]) + bias
                gated.append((u[rows] * s).astype(BF16))
            y_g = _dot(jnp.concatenate(gated, axis=0), wout_ref[cols, :])
            y = y_g if y is None else y + y_g
            if gi + 2 < GM_GROUPS:
                event += 1
                between(event, y_g)
        assert event == len(GMLP_PIECES) - 1
        return y

    def tail_piece(sub, h, y, p, after=None):
        r = slice(starts[p], starts[p] + GMLP_PIECES[p])
        res = ALPHA * h[r] + m[2:3] * y[r]
        if after is not None:
            res = _order_after(res, after)
        lo = sub * GMLP_SUB + starts[p]
        o_ref[lo:lo + GMLP_PIECES[p], :] = _layer_norm(res, lng_ref[...], lnb_ref[...])

    assert nsub == 2
    (h0, a0), (h1, a1) = head(0), head(1)
    v_project(0, a0, lambda j, raw: None)
    v_project(1, a1, lambda j, raw: normalise_piece(0, j, raw))
    y0 = mix(0, a0, lambda e, result: normalise_piece(1, e, result))
    y1 = mix(1, a1, lambda e, result: tail_piece(0, h0, y0, e, result))
    for p in range(len(GMLP_PIECES)):
        tail_piece(1, h1, y1, p)


def _gmlp(h2d, mods, consts, tiles_per_batch):
    n, d = h2d.shape
    tm = GMLP_TILE
    width = consts[5].shape[0]
    row = lambda i: (i, 0)
    return pl.pallas_call(
        _gmlp_kernel,
        grid=(n // tm,),
        in_specs=[
            pl.BlockSpec((tm, d), row),
            pl.BlockSpec((1,) + mods.shape[1:], lambda i: (i // tiles_per_batch, 0, 0)),
        ] + [_const_spec(t.shape) for t in consts],
        out_specs=pl.BlockSpec((tm, d), row),
        out_shape=jax.ShapeDtypeStruct((n, d), F32),
        scratch_shapes=[pltpu.VMEM((tm // GMLP_SUB, GMLP_SUB, width), F32),
                        pltpu.VMEM((tm // GMLP_SUB, GMLP_SUB, width), BF16)],
        compiler_params=_params("parallel"),
        name="gmlp_mixer",
    )(h2d, mods, *consts)


def kernel(x, c, ctx, c_ctx, mod_w, mod_b, ln_g, ln_b, gla_w_in, gla_w_decay, gla_b_decay, gla_norm_g,
           gla_w_out, gm_w_in, gm_ln_g, gm_ln_b, gm_w_s, gm_b_s, gm_w_out, ffn_w_gate, ffn_w_up, ffn_w_down):
    bsz, l, d = x.shape
    lc = ctx.shape[1]
    n = bsz * l
    assert bsz + 1 <= COND_ROWS
    dk = gla_w_decay.shape[-1]
    dv = gla_w_out.shape[1]
    q_scale = (dk // GLA_HEADS) ** -0.5
    vec = lambda t: t.reshape(1, -1)

    cond = jnp.concatenate([c, c_ctx[None], jnp.zeros((COND_ROWS - bsz - 1, d), F32)], axis=0)
    mods = _adaln(cond, mod_w, mod_b).reshape(DEPTH, COND_ROWS, 6, d)

    w_in = gla_w_in[0]
    o_a = dk + dv
    o_q = o_a + 2 * GLA_RANK
    w_k, w_v, w_a = w_in[:, :dk], w_in[:, dk:o_a], w_in[:, o_a:o_q]
    w_q, w_r = w_in[:, o_q:o_q + dk], w_in[:, o_q + dk:]
    w_a = jnp.pad(w_a, ((0, 0), (0, LANES - 2 * GLA_RANK)))
    w_lat = jnp.concatenate([w_k, w_q, w_a, w_v, w_r], axis=1).astype(BF16)
    w_ctx = jnp.concatenate([w_k, w_a, w_v], axis=1).astype(BF16)
    wdec = jnp.stack([jnp.pad(gla_w_decay[0, rev], ((rev * GLA_RANK, LANES - (rev + 1) * GLA_RANK), (0, 0)))
                      for rev in (0, 1)]).astype(BF16)
    bdec = gla_b_decay[0].reshape(2, 1, dk)

    tpb = l // PROJ_TILE
    later = (gla_w_out, gm_w_in, gm_w_out, ffn_w_gate, ffn_w_up, ffn_w_down)
    ((v, gate, qin_f, kin_f, kst_f, dec_f, qin_b, kin_b, kst_b, dec_b),
     (w_out, gm_in, gm_out, wg0, wg1, wu0, wu1, wd0, wd1)) = _gla_proj(
        x.reshape(n, d), mods[0], w_lat, wdec, bdec, lambda i: i // tpb, True, PROJ_TILE, dk, dv, q_scale,
        cast=later)
    (vc, kstc_f, decc_f, kstc_b, decc_b), _ = _gla_proj(
        ctx.reshape(bsz * lc, d), mods[0], w_ctx, wdec, bdec, lambda i: bsz, False, lc, dk, dv, q_scale)
    b3 = lambda t: t.reshape(bsz, l, -1)
    v3, vc3 = b3(v), vc.reshape(bsz, lc, dv)
    o_b = _gla_scan(b3(qin_b), b3(kin_b), kst_b, v3, dec_b, kstc_b, vc3, decc_b, True)
    o = _gla_scan(b3(qin_f), b3(kin_f), kst_f, v3, dec_f, kstc_f, vc3, decc_f, False, prev=o_b)

    ffn_w = ((wg0, wu0, wd0), (wg1, wu1, wd1))
    ffn_consts = lambda i: ffn_w[i] + (vec(ln_g[i, 1]), vec(ln_b[i, 1]))
    gla_consts = (vec(gla_norm_g[0]), w_out, vec(ln_g[0, 0]), vec(ln_b[0, 0]))
    tpf = l // FFN_TILE
    h = _ffn((o.reshape(n, dv), gate, x.reshape(n, d)), mods[0], gla_consts + ffn_consts(0), tpf, True)

    bs = jnp.broadcast_to(gm_b_s[0].T[:, :, None], (GM_GROUPS, GM_CHUNK, LANES))
    gm_consts = (gm_in, vec(gm_ln_g[0]), vec(gm_ln_b[0]), gm_w_s[0].astype(BF16), bs,
                 gm_out, vec(ln_g[1, 0]), vec(ln_b[1, 0]))
    h = _gmlp(h, mods[1], gm_consts, l // GMLP_TILE)
    h = _ffn((h,), mods[1], ffn_consts(1), tpf, False)
    return h.reshape(bsz, l, d)
```

```python
import functools

import jax
import jax.numpy as jnp
from jax import lax
from jax.experimental import pallas as pl
from jax.experimental.pallas import tpu as pltpu

F32 = jnp.float32
BF16 = jnp.bfloat16

DEPTH = 2
ALPHA = (2 * DEPTH) ** 0.25
LOG2E = 1.4426950408889634
GLA_HEADS = 4
GLA_RANK = 16
GLA_GATE_NORM = 16.0
GLA_CHUNK = 64
GM_GROUPS = 4
GM_CHUNK = 128
COND_ROWS = 8
ADALN_COL_BLOCKS = 4
LANES = 128
VMEM_LIMIT = 56 * 1024 * 1024

PROJ_TILE = 512
SCAN_BLOCK = 1024
DECAY_BLOCK = 256
KST_GROUP = 2
FFN_TILE = 1024
FFN_SUB = 512
FFN_HEAD_AT = (3, 6)
FFN_TAIL_PIECES = 8
FFN_COLS = 256
GMLP_TILE = 1024
GMLP_SUB = 512
GMLP_COLS = 512
GMLP_PIECES = (96, 96, 80, 80, 80, 80)


def _dot(a, b):
    return jnp.dot(a, b, preferred_element_type=F32)


def _layer_norm(x, g, b, eps=1e-5):
    mu = jnp.mean(x, axis=-1, keepdims=True)
    xc = x - mu
    var = jnp.mean(xc * xc, axis=-1, keepdims=True)
    return xc * lax.rsqrt(var + eps) * g + b


def _silu(x):
    return x * jax.nn.sigmoid(x)


def _lane_fold(t):
    acc = t[:, 0:LANES]
    for j in range(1, t.shape[1] // LANES):
        acc = acc + t[:, j * LANES:(j + 1) * LANES]
    return acc


def _order_after(x, dep):
    bits = pltpu.bitcast(dep[0:x.shape[0], 0:LANES], jnp.uint32)
    half = jnp.uint32(16)
    zero = pltpu.bitcast(lax.shift_right_logical(lax.shift_right_logical(bits, half), half), F32)
    return jnp.concatenate([x[:, 0:LANES] + zero, x[:, LANES:]], axis=1)


def _params(*sem):
    return pltpu.CompilerParams(dimension_semantics=sem, vmem_limit_bytes=VMEM_LIMIT)


def _const_spec(shape):
    nd = len(shape)
    return pl.BlockSpec(shape, lambda *_: (0,) * nd, pipeline_mode=pl.Buffered(1))


def _adaln_kernel(cond_ref, w_ref, b_ref, o_ref):
    s = _silu(cond_ref[...]).astype(BF16)
    o_ref[0] = _dot(s, w_ref[0].astype(BF16)) + b_ref[0]


def _adaln(cond, mod_w, mod_b):
    depth, d, n = mod_w.shape
    tn = n // ADALN_COL_BLOCKS
    return pl.pallas_call(
        _adaln_kernel,
        grid=(depth, n // tn),
        in_specs=[
            pl.BlockSpec((COND_ROWS, d), lambda i, j: (0, 0)),
            pl.BlockSpec((1, d, tn), lambda i, j: (i, 0, j)),
            pl.BlockSpec((1, 1, tn), lambda i, j: (i, 0, j)),
        ],
        out_specs=pl.BlockSpec((1, COND_ROWS, tn), lambda i, j: (i, 0, j)),
        out_shape=jax.ShapeDtypeStruct((depth, COND_ROWS, n), F32),
        compiler_params=_params("arbitrary", "arbitrary"),
        name="adaln",
    )(cond, mod_w, mod_b.reshape(depth, 1, n))


def _log_sigmoid(z):
    return jnp.minimum(z, 0.0) - jnp.log(1.0 + jnp.exp2(jnp.abs(z) * -LOG2E))


def _chunk_tri(n, reverse):
    row = lax.broadcasted_iota(jnp.int32, (n, n), 0)
    col = lax.broadcasted_iota(jnp.int32, (n, n), 1)
    shift = GLA_CHUNK.bit_length() - 1
    same = jnp.right_shift(row, shift) == jnp.right_shift(col, shift)
    return jnp.where(same & ((col >= row) if reverse else (col <= row)), 1.0, 0.0).astype(BF16)


def _gla_proj_kernel(x_ref, mod_ref, w_ref, wdec_ref, bdec_ref, *refs, latent, dk, dv, q_scale, cast_layers):
    n_src, n_dst = len(cast_layers), sum(cast_layers)
    dsts = iter(refs[len(refs) - n_dst:])
    for src, layers in zip(refs[:n_src], cast_layers):
        for layer in range(layers):
            next(dsts)[0] = src[layer, 0].astype(BF16)
    out_refs = refs[n_src:len(refs) - n_dst]
    m = mod_ref[0]
    a = (x_ref[...] * (1.0 + m[1:2]) + m[0:1]).astype(BF16)
    tm = a.shape[0]
    nblk = tm // DECAY_BLOCK
    ncb = DECAY_BLOCK // GLA_CHUNK
    o_a = 2 * dk if latent else dk
    o_v = o_a + LANES
    if latent:
        v_ref, gate_ref = out_refs[:2]
        dir_refs = (out_refs[2:6], out_refs[6:10])
    else:
        v_ref = out_refs[0]
        dir_refs = ((None, None) + tuple(out_refs[1:3]), (None, None) + tuple(out_refs[3:5]))
    k = _dot(a, w_ref[:, 0:dk])
    if latent:
        q = _dot(a, w_ref[:, dk:2 * dk]) * q_scale
    a_lr = _dot(a, w_ref[:, o_a:o_a + LANES]).astype(BF16)
    z = [_dot(a_lr, wdec_ref[rev]) + bdec_ref[rev] for rev in (0, 1)]
    v_ref[...] = _dot(a, w_ref[:, o_v:o_v + dv]).astype(BF16)
    g_parts = []
    for rev in (0, 1):
        g = _log_sigmoid(z[rev]) * (1.0 / GLA_GATE_NORM)
        g_hi = g.astype(BF16)
        g_parts.append((g_hi, (g - g_hi.astype(F32)).astype(BF16)))
    tris = (_chunk_tri(DECAY_BLOCK, False), _chunk_tri(DECAY_BLOCK, True))
    b = {}
    for rev in (0, 1):
        for blk in range(nblk):
            rows = slice(blk * DECAY_BLOCK, (blk + 1) * DECAY_BLOCK)
            b[rev, blk] = _dot(tris[rev], g_parts[rev][0][rows]) + _dot(tris[rev], g_parts[rev][1][rows])
    if latent:
        gate_ref[...] = _silu(_dot(a, w_ref[:, o_v + dv:o_v + 2 * dv])).astype(BF16)
    for rev in (0, 1):
        qin_ref, kin_ref, kst_ref, dec_ref = dir_refs[rev]
        group_chunks = dec_ref.shape[1]
        for blk in range(nblk):
            rows = slice(blk * DECAY_BLOCK, (blk + 1) * DECAY_BLOCK)
            bb = b[rev, blk] * LOG2E
            if latent:
                qin_ref[rows, :] = (q[rows] * jnp.exp2(bb)).astype(BF16)
                kin_ref[rows, :] = (k[rows] * jnp.exp2(-bb)).astype(BF16)
            ends, kst = [], []
            for c in range(ncb):
                lo = c * GLA_CHUNK
                b_c = bb[lo:lo + GLA_CHUNK]
                b_end = b_c[0:1] if rev else b_c[GLA_CHUNK - 1:GLA_CHUNK]
                kst.append(k[blk * DECAY_BLOCK + lo:blk * DECAY_BLOCK + lo + GLA_CHUNK] * jnp.exp2(b_end - b_c))
                ends.append(b_end)
            for p in range(ncb // KST_GROUP):
                grp = jnp.concatenate(kst[p * KST_GROUP:(p + 1) * KST_GROUP], axis=0)
                kst_ref[blk * (ncb // KST_GROUP) + p] = grp.T.astype(BF16)
            first = blk * ncb
            dec_ref[first // group_chunks, first % group_chunks:first % group_chunks + ncb, :] = jnp.exp2(
                jnp.concatenate(ends, axis=0))


def _gla_proj(x2d, mods, w, wdec, bdec, mod_row, latent, tm, dk, dv, q_scale, cast=()):
    n, d = x2d.shape
    steps = n // tm
    group = min(SCAN_BLOCK, tm)
    row = lambda i: (i, 0)
    row3 = lambda i: (i, 0, 0)
    slabs = [t.reshape(t.shape[0], steps, t.shape[1] // steps, t.shape[2]) for t in cast]
    slab_in = [pl.BlockSpec((t.shape[0], 1) + t.shape[2:], lambda i: (0, i, 0, 0)) for t in slabs]
    slab_out = [(jax.ShapeDtypeStruct(t.shape[1:], BF16), pl.BlockSpec((1,) + t.shape[2:], row3))
                for t in slabs for _ in range(t.shape[0])]
    tok = lambda c: (jax.ShapeDtypeStruct((n, c), BF16), pl.BlockSpec((tm, c), row))
    kst = (jax.ShapeDtypeStruct((n // LANES, dk, LANES), BF16),
           pl.BlockSpec((tm // LANES, dk, LANES), row3))
    dec = (jax.ShapeDtypeStruct((n // group, group // GLA_CHUNK, dk), F32),
           pl.BlockSpec((tm // group, group // GLA_CHUNK, dk), row3))
    per_dir = [tok(dk), tok(dk), kst, dec] if latent else [kst, dec]
    outs = ([tok(dv), tok(dv)] if latent else [tok(dv)]) + per_dir + per_dir
    res = pl.pallas_call(
        functools.partial(_gla_proj_kernel, latent=latent, dk=dk, dv=dv, q_scale=q_scale,
                          cast_layers=tuple(t.shape[0] for t in cast)),
        grid=(steps,),
        in_specs=[
            pl.BlockSpec((tm, d), row),
            pl.BlockSpec((1,) + mods.shape[1:], lambda i: (mod_row(i), 0, 0)),
            _const_spec(w.shape), _const_spec(wdec.shape), _const_spec(bdec.shape),
        ] + slab_in,
        out_specs=[o[1] for o in outs] + [o[1] for o in slab_out],
        out_shape=[o[0] for o in outs] + [o[0] for o in slab_out],
        compiler_params=_params("parallel"),
        name="gla_in_proj" if latent else "gla_ctx_proj",
    )(x2d, mods, w, wdec, bdec, *slabs)
    n_out = len(outs)
    shapes = [t.shape[1:] for t in cast for _ in range(t.shape[0])]
    return res[:n_out], [r.reshape(s) for r, s in zip(res[n_out:], shapes)]


def _decay_columns(dec_row, dvh):
    dkh = dec_row.shape[1]
    dcol = jnp.broadcast_to(dec_row, (dkh, dkh)).T
    return jnp.concatenate([dcol] * (dvh // dkh), axis=1)


def _gla_scan_kernel(*refs, reverse, add_prev):
    qin_ref, kin_ref, kst_ref, v_ref, dec_ref, kstc_ref, vc_ref, decc_ref = refs[:8]
    if add_prev:
        prev_ref, o_ref, s_ref = refs[8:]
    else:
        o_ref, s_ref = refs[8:]
    dkh, dvh = s_ref.shape[1:]
    heads = range(GLA_HEADS)
    ks = [slice(h * dkh, (h + 1) * dkh) for h in heads]
    vs = [slice(h * dvh, (h + 1) * dvh) for h in heads]

    def order(nc):
        return list(range(nc - 1, -1, -1) if reverse else range(nc))

    def rows(c):
        return slice(c * GLA_CHUNK, (c + 1) * GLA_CHUNK)

    lane_chunk = lax.broadcasted_iota(jnp.int32, (dkh, LANES), 1) // GLA_CHUNK

    def state_update(kst_g_ref, v_g_ref, c, h):
        g = c // KST_GROUP
        keys = jnp.where(lane_chunk == c % KST_GROUP, kst_g_ref[g, ks[h], :], jnp.zeros((), BF16))
        return _dot(keys, v_g_ref[0, g * LANES:(g + 1) * LANES, vs[h]])

    @pl.when(pl.program_id(1) == 0)
    def _():
        chunks = order(vc_ref.shape[1] // GLA_CHUNK)
        upd = {(c, h): state_update(kstc_ref, vc_ref, c, h) for c in chunks for h in heads}
        for h in heads:
            s = jnp.zeros((dkh, dvh), F32)
            for c in chunks:
                s = s * _decay_columns(decc_ref[0, c:c + 1, ks[h]], dvh) + upd[c, h]
            s_ref[h] = s

    chunks = order(v_ref.shape[1] // GLA_CHUNK)
    pairs = [(c, h) for c in chunks for h in heads]
    r64 = lax.broadcasted_iota(jnp.int32, (GLA_CHUNK, GLA_CHUNK), 0)
    c64 = lax.broadcasted_iota(jnp.int32, (GLA_CHUNK, GLA_CHUNK), 1)
    mask = (c64 >= r64) if reverse else (c64 <= r64)
    att = {(c, h): lax.dot_general(qin_ref[0, rows(c), ks[h]], kin_ref[0, rows(c), ks[h]],
                                   (((1,), (1,)), ((), ())), preferred_element_type=F32) for c, h in pairs}
    upd = {(c, h): state_update(kst_ref, v_ref, c, h) for c, h in pairs}
    dcol = {(c, h): _decay_columns(dec_ref[0, c:c + 1, ks[h]], dvh) for c, h in pairs}
    lhs = {(c, h): jnp.concatenate([qin_ref[0, rows(c), ks[h]], jnp.where(mask, att[c, h], 0.0).astype(BF16)],
                                   axis=1) for c, h in pairs}
    state = [s_ref[h] for h in heads]
    for c in chunks:
        outs = []
        for h in heads:
            rhs = jnp.concatenate([state[h].astype(BF16), v_ref[0, rows(c), vs[h]]], axis=0)
            outs.append(_dot(lhs[c, h], rhs))
            state[h] = state[h] * dcol[c, h] + upd[c, h]
        o = jnp.concatenate(outs, axis=1)
        if add_prev:
            o = o + prev_ref[0, rows(c), :].astype(F32)
        o_ref[0, rows(c), :] = o.astype(o_ref.dtype)
    for h in heads:
        s_ref[h] = state[h]


def _gla_scan(qin, kin, kst, v, dec, kstc, vc, decc, reverse, prev=None):
    bsz, l, dk = qin.shape
    dv = v.shape[2]
    lc = vc.shape[1]
    tb = SCAN_BLOCK
    nblk = l // tb
    ncb = tb // GLA_CHUNK

    def pos(j):
        return (nblk - 1 - j) if reverse else j

    blk = lambda b, j: (b, pos(j), 0)
    flat = lambda b, j: (b * nblk + pos(j), 0, 0)
    ctx = lambda b, j: (b, 0, 0)
    in_specs = [
        pl.BlockSpec((1, tb, dk), blk), pl.BlockSpec((1, tb, dk), blk),
        pl.BlockSpec((tb // LANES, dk, LANES), flat), pl.BlockSpec((1, tb, dv), blk),
        pl.BlockSpec((1, ncb, dk), flat),
        pl.BlockSpec((lc // LANES, dk, LANES), ctx), pl.BlockSpec((1, lc, dv), ctx),
        pl.BlockSpec((1, lc // GLA_CHUNK, dk), ctx),
    ]
    args = [qin, kin, kst, v, dec.reshape(bsz * nblk, ncb, dk), kstc, vc, decc]
    if prev is not None:
        in_specs.append(pl.BlockSpec((1, tb, dv), blk))
        args.append(prev)
    return pl.pallas_call(
        functools.partial(_gla_scan_kernel, reverse=reverse, add_prev=prev is not None),
        grid=(bsz, nblk),
        in_specs=in_specs,
        out_specs=pl.BlockSpec((1, tb, dv), blk),
        out_shape=jax.ShapeDtypeStruct((bsz, l, dv), BF16),
        scratch_shapes=[pltpu.VMEM((GLA_HEADS, dk // GLA_HEADS, dv // GLA_HEADS), F32)],
        compiler_params=_params("arbitrary", "arbitrary"),
        name="gla_scan_bwd" if reverse else "gla_scan_fwd",
    )(*args)


def _ffn_kernel(*refs, gla_prologue):
    if gla_prologue:
        (o_ref, gate_ref, x_ref, mod_ref, ng_ref, wout_ref, lng0_ref, lnb0_ref,
         wg_ref, wu_ref, wd_ref, lng_ref, lnb_ref, out_ref, hid_ref) = refs
    else:
        h_ref, mod_ref, wg_ref, wu_ref, wd_ref, lng_ref, lnb_ref, out_ref, hid_ref = refs
    m = mod_ref[0]
    hidden = wg_ref.shape[1]
    nsub = out_ref.shape[0] // FFN_SUB

    def head(lo, size):
        rows = slice(lo, lo + size)
        if gla_prologue:
            o = o_ref[rows, :].astype(F32)
            dvh = o.shape[1] // GLA_HEADS
            normed = []
            for hd in range(GLA_HEADS):
                o_h = o[:, hd * dvh:(hd + 1) * dvh]
                ms = jnp.mean(o_h * o_h, axis=-1, keepdims=True)
                normed.append(o_h * lax.rsqrt(ms + 1e-6) * ng_ref[...])
            gated = (jnp.concatenate(normed, axis=1) * gate_ref[rows, :].astype(F32)).astype(BF16)
            y = _dot(gated, wout_ref[...])
            h = _layer_norm(ALPHA * x_ref[rows, :] + m[2:3] * y, lng0_ref[...], lnb0_ref[...])
        else:
            h = h_ref[rows, :]
        af = h * (1.0 + m[4:5]) + m[3:4]
        return h, af.astype(BF16), _lane_fold(af)

    def tail(lo, h, f, after=None):
        res = ALPHA * h + m[5:6] * f
        if after is not None:
            res = _order_after(res, after)
        out_ref[lo:lo + h.shape[0], :] = _layer_norm(res, lng_ref[...], lnb_ref[...])

    head_rows = FFN_SUB // len(FFN_HEAD_AT)
    tail_rows = FFN_SUB // FFN_TAIL_PIECES
    h, a, _ = head(0, FFN_SUB)
    pending = None
    for sub in range(nsub):
        nxt, done_tail = [], 0
        for j, lo in enumerate(range(0, hidden, FFN_COLS)):
            piece = None
            if sub + 1 < nsub and j in FFN_HEAD_AT:
                piece = head((sub + 1) * FFN_SUB + FFN_HEAD_AT.index(j) * head_rows, head_rows)
                nxt.append(piece)
            g = _dot(a, wg_ref[:, lo:lo + FFN_COLS])
            u = _dot(a, wu_ref[:, lo:lo + FFN_COLS])
            prod = _silu(g) * u
            if piece is not None:
                prod = jnp.concatenate([_order_after(prod[0:head_rows], piece[2]), prod[head_rows:]], axis=0)
            hid_ref[sub, :, lo:lo + FFN_COLS] = prod.astype(BF16)
            if pending is not None and piece is None and done_tail < FFN_TAIL_PIECES:
                p_lo, p_h, p_f = pending
                r = slice(done_tail * tail_rows, (done_tail + 1) * tail_rows)
                tail(p_lo + done_tail * tail_rows, p_h[r], p_f[r], after=g)
                done_tail += 1
        assert pending is None or done_tail == FFN_TAIL_PIECES
        if sub + 1 < nsub:
            pending = (sub * FFN_SUB, h, _dot(hid_ref[sub], wd_ref[...]))
            h = jnp.concatenate([p[0] for p in nxt], axis=0)
            a = jnp.concatenate([p[1] for p in nxt], axis=0)
        else:
            half = FFN_SUB // 2
            f_rows = [_dot(hid_ref[sub, r0:r0 + half, :], wd_ref[...]) for r0 in (0, half)]
            for r0, f in zip((0, half), f_rows):
                tail(sub * FFN_SUB + r0, h[r0:r0 + half], f)


def _ffn(acts, mods, consts, tiles_per_batch, gla_prologue):
    n, d = acts[0].shape
    tm = FFN_TILE
    hidden = consts[-5].shape[1]
    row = lambda i: (i, 0)
    mod_spec = pl.BlockSpec((1,) + mods.shape[1:], lambda i: (i // tiles_per_batch, 0, 0))
    in_specs = ([pl.BlockSpec((tm, t.shape[1]), row) for t in acts] + [mod_spec]
                + [_const_spec(t.shape) for t in consts])
    return pl.pallas_call(
        functools.partial(_ffn_kernel, gla_prologue=gla_prologue),
        grid=(n // tm,),
        in_specs=in_specs,
        out_specs=pl.BlockSpec((tm, d), row),
        out_shape=jax.ShapeDtypeStruct((n, d), F32),
        scratch_shapes=[pltpu.VMEM((tm // FFN_SUB, FFN_SUB, hidden), BF16)],
        compiler_params=_params("parallel"),
        name="gla_out_ffn" if gla_prologue else "swiglu_ffn",
    )(*acts, mods, *consts)


def _gmlp_kernel(h_ref, mod_ref, win_ref, vg_ref, vb_ref, ws_ref, bs_ref, wout_ref, lng_ref, lnb_ref,
                 o_ref, z_scr, vn_scr):
    m = mod_ref[0]
    width = z_scr.shape[2]
    gw = width // GM_GROUPS
    nsub = o_ref.shape[0] // GMLP_SUB
    chunks = list(range(0, width, GMLP_COLS))
    starts = [sum(GMLP_PIECES[:i]) for i in range(len(GMLP_PIECES))]
    assert sum(GMLP_PIECES) == GMLP_SUB and len(GMLP_PIECES) == len(chunks) and GM_GROUPS >= 3

    def gelu(z):
        return 0.5 * z * (1.0 + lax.erf(z * (0.5 ** 0.5)))

    def head(sub):
        h = h_ref[sub * GMLP_SUB:(sub + 1) * GMLP_SUB, :]
        return h, (h * (1.0 + m[1:2]) + m[0:1]).astype(BF16)

    def v_project(sub, a, between):
        for j, lo in enumerate(chunks):
            raw = _dot(a, win_ref[:, width + lo:width + lo + GMLP_COLS])
            z_scr[sub, :, lo:lo + GMLP_COLS] = raw
            between(j, raw)

    def normalise_piece(sub, p, after):
        r = slice(starts[p], starts[p] + GMLP_PIECES[p])
        v = gelu(_order_after(z_scr[sub, r, :], after))
        vn_scr[sub, r, :] = _layer_norm(v, vg_ref[...], vb_ref[...]).astype(BF16)

    def mix(sub, a, between, row_halves=False):
        event = 0
        u_raw = _dot(a, win_ref[:, 0:gw])
        between(event, u_raw)
        y = None
        for gi in range(GM_GROUPS):
            cols = slice(gi * gw, (gi + 1) * gw)
            u = gelu(u_raw)
            if gi + 1 < GM_GROUPS:
                u_raw = _dot(a, win_ref[:, (gi + 1) * gw:(gi + 2) * gw])
                event += 1
                between(event, u_raw)
            bias = jnp.concatenate([bs_ref[gi]] * (gw // LANES), axis=1)
            gated = []
            for t in range(GMLP_SUB // GM_CHUNK):
                rows = slice(t * GM_CHUNK, (t + 1) * GM_CHUNK)
                s = _dot(ws_ref[gi], vn_scr[sub, rows, cols]) + bias
                gated.append((u[rows] * s).astype(BF16))
            packed = jnp.concatenate(gated, axis=0)
            if row_halves and gi + 1 == GM_GROUPS:
                assert event == len(GMLP_PIECES) - 1
                half = GMLP_SUB // 2
                return [(r0, y[r0:r0 + half] + _dot(packed[r0:r0 + half], wout_ref[cols, :])) for r0 in (0, half)]
            y_g = _dot(packed, wout_ref[cols, :])
            y = y_g if y is None else y + y_g
            if gi + 2 < GM_GROUPS:
                event += 1
                between(event, y_g)
        assert event == len(GMLP_PIECES) - 1
        return y

    def tail_piece(sub, h, y, p, after=None):
        r = slice(starts[p], starts[p] + GMLP_PIECES[p])
        res = ALPHA * h[r] + m[2:3] * y[r]
        if after is not None:
            res = _order_after(res, after)
        lo = sub * GMLP_SUB + starts[p]
        o_ref[lo:lo + GMLP_PIECES[p], :] = _layer_norm(res, lng_ref[...], lnb_ref[...])

    assert nsub == 2
    (h0, a0), (h1, a1) = head(0), head(1)
    v_project(0, a0, lambda j, raw: None)
    v_project(1, a1, lambda j, raw: normalise_piece(0, j, raw))
    y0 = mix(0, a0, lambda e, result: normalise_piece(1, e, result))
    for r0, y_rows in mix(1, a1, lambda e, result: tail_piece(0, h0, y0, e, result), row_halves=True):
        n = y_rows.shape[0]
        res = ALPHA * h1[r0:r0 + n] + m[2:3] * y_rows
        o_ref[GMLP_SUB + r0:GMLP_SUB + r0 + n, :] = _layer_norm(res, lng_ref[...], lnb_ref[...])


def _gmlp(h2d, mods, consts, tiles_per_batch):
    n, d = h2d.shape
    tm = GMLP_TILE
    width = consts[5].shape[0]
    row = lambda i: (i, 0)
    return pl.pallas_call(
        _gmlp_kernel,
        grid=(n // tm,),
        in_specs=[
            pl.BlockSpec((tm, d), row),
            pl.BlockSpec((1,) + mods.shape[1:], lambda i: (i // tiles_per_batch, 0, 0)),
        ] + [_const_spec(t.shape) for t in consts],
        out_specs=pl.BlockSpec((tm, d), row),
        out_shape=jax.ShapeDtypeStruct((n, d), F32),
        scratch_shapes=[pltpu.VMEM((tm // GMLP_SUB, GMLP_SUB, width), F32),
                        pltpu.VMEM((tm // GMLP_SUB, GMLP_SUB, width), BF16)],
        compiler_params=_params("parallel"),
        name="gmlp_mixer",
    )(h2d, mods, *consts)


def kernel(x, c, ctx, c_ctx, mod_w, mod_b, ln_g, ln_b, gla_w_in, gla_w_decay, gla_b_decay, gla_norm_g,
           gla_w_out, gm_w_in, gm_ln_g, gm_ln_b, gm_w_s, gm_b_s, gm_w_out, ffn_w_gate, ffn_w_up, ffn_w_down):
    bsz, l, d = x.shape
    lc = ctx.shape[1]
    n = bsz * l
    assert bsz + 1 <= COND_ROWS
    dk = gla_w_decay.shape[-1]
    dv = gla_w_out.shape[1]
    q_scale = (dk // GLA_HEADS) ** -0.5
    vec = lambda t: t.reshape(1, -1)

    cond = jnp.concatenate([c, c_ctx[None], jnp.zeros((COND_ROWS - bsz - 1, d), F32)], axis=0)
    mods = _adaln(cond, mod_w, mod_b).reshape(DEPTH, COND_ROWS, 6, d)

    w_in = gla_w_in[0]
    o_a = dk + dv
    o_q = o_a + 2 * GLA_RANK
    w_k, w_v, w_a = w_in[:, :dk], w_in[:, dk:o_a], w_in[:, o_a:o_q]
    w_q, w_r = w_in[:, o_q:o_q + dk], w_in[:, o_q + dk:]
    w_a = jnp.pad(w_a, ((0, 0), (0, LANES - 2 * GLA_RANK)))
    w_lat = jnp.concatenate([w_k, w_q, w_a, w_v, w_r], axis=1).astype(BF16)
    w_ctx = jnp.concatenate([w_k, w_a, w_v], axis=1).astype(BF16)
    wdec = jnp.zeros((2, LANES, dk), F32)
    wdec = wdec.at[0, :GLA_RANK].set(gla_w_decay[0, 0]).at[1, GLA_RANK:2 * GLA_RANK].set(gla_w_decay[0, 1])
    wdec = wdec.astype(BF16)
    bdec = gla_b_decay[0].reshape(2, 1, dk)

    tpb = l // PROJ_TILE
    later = (gla_w_out, gm_w_in, gm_w_out, ffn_w_gate, ffn_w_up, ffn_w_down)
    ((v, gate, qin_f, kin_f, kst_f, dec_f, qin_b, kin_b, kst_b, dec_b),
     (w_out, gm_in, gm_out, wg0, wg1, wu0, wu1, wd0, wd1)) = _gla_proj(
        x.reshape(n, d), mods[0], w_lat, wdec, bdec, lambda i: i // tpb, True, PROJ_TILE, dk, dv, q_scale,
        cast=later)
    (vc, kstc_f, decc_f, kstc_b, decc_b), _ = _gla_proj(
        ctx.reshape(bsz * lc, d), mods[0], w_ctx, wdec, bdec, lambda i: bsz, False, lc, dk, dv, q_scale)
    b3 = lambda t: t.reshape(bsz, l, -1)
    v3, vc3 = b3(v), vc.reshape(bsz, lc, dv)
    o_b = _gla_scan(b3(qin_b), b3(kin_b), kst_b, v3, dec_b, kstc_b, vc3, decc_b, True)
    o = _gla_scan(b3(qin_f), b3(kin_f), kst_f, v3, dec_f, kstc_f, vc3, decc_f, False, prev=o_b)

    ffn_w = ((wg0, wu0, wd0), (wg1, wu1, wd1))
    ffn_consts = lambda i: ffn_w[i] + (vec(ln_g[i, 1]), vec(ln_b[i, 1]))
    gla_consts = (vec(gla_norm_g[0]), w_out, vec(ln_g[0, 0]), vec(ln_b[0, 0]))
    tpf = l // FFN_TILE
    h = _ffn((o.reshape(n, dv), gate, x.reshape(n, d)), mods[0], gla_consts + ffn_consts(0), tpf, True)

    bs = jnp.broadcast_to(gm_b_s[0].T[:, :, None], (GM_GROUPS, GM_CHUNK, LANES))
    gm_consts = (gm_in, vec(gm_ln_g[0]), vec(gm_ln_b[0]), gm_w_s[0].astype(BF16), bs,
                 gm_out, vec(ln_g[1, 0]), vec(ln_b[1, 0]))
    h = _gmlp(h, mods[1], gm_consts, l // GMLP_TILE)
    h = _ffn((h,), mods[1], ffn_consts(1), tpf, False)
    return h.reshape(bsz, l, d)
```
